```python
import math
import jax, jax.numpy as jnp
from jax import lax
import numpy as np

D_MODEL = 1024
BATCH = 8
SEQ = 4096
DEPTH = 4

HEAD_DIM = 64
N_HEADS = D_MODEL // HEAD_DIM
H_DSA = N_HEADS // 2
H_FOX = N_HEADS - H_DSA
D_LATENT = D_MODEL // 8
H_IDX = 8
D_IDX = 64
TOPK_MAX = 256
Q_BLOCK = 128
D_FF = ((8 * D_MODEL // 3 + 127) // 128) * 128
N_BUCKETS = 32
MAX_DISTANCE = 128
EPS = 1e-6
NEG = -1e30

kernel_name = 'hybrid_dsa_fox_macaron'


def rms_norm(x, g):
    xf = x.astype(jnp.float32)
    y = xf * lax.rsqrt(jnp.mean(xf * xf, axis=-1, keepdims=True) + EPS)
    return (y * g.astype(jnp.float32)).astype(x.dtype)


def swiglu(x, w_gate, w_up, w_down):
    return (jax.nn.silu(x @ w_gate) * (x @ w_up)) @ w_down


def t5_bucket(dist):
    n = jnp.maximum(dist, 0)
    max_exact = N_BUCKETS // 2
    nf = jnp.maximum(n, 1).astype(jnp.float32)
    large = max_exact + (jnp.log(nf / max_exact) / math.log(MAX_DISTANCE / max_exact)
                         * (N_BUCKETS - max_exact)).astype(jnp.int32)
    large = jnp.minimum(large, N_BUCKETS - 1)
    return jnp.where(n < max_exact, n, large)


def to_blocks(a):
    b, s = a.shape[:2]
    return jnp.moveaxis(a.reshape((b, s // Q_BLOCK, Q_BLOCK) + a.shape[2:]), 1, 0)


def from_blocks(a):
    nb, b, qb = a.shape[:3]
    return jnp.moveaxis(a, 0, 1).reshape((b, nb * qb) + a.shape[3:])


def dsa_attention(q, c_kv, q_idx, k_idx, w_idx, w_uk, w_uv, rel_bias):
    b, s = q.shape[:2]
    k_sel = min(TOPK_MAX, s // 4)
    q_lat = jnp.einsum('bshd,chd->bshc', q, w_uk)
    key_pos = jnp.arange(s)

    def block(args):
        blk, ql, qi, wi = args
        t = blk * Q_BLOCK + jnp.arange(Q_BLOCK)
        rel = jnp.einsum('bqhd,bsd->bqhs', qi, k_idx).astype(jnp.float32) * (D_IDX ** -0.5)
        score = jnp.einsum('bqh,bqhs->bqs', wi.astype(jnp.float32) * (H_IDX ** -0.5), jax.nn.relu(rel))
        score = jnp.where(key_pos[None, None, :] <= t[None, :, None], score, NEG)
        _, idx = lax.top_k(score, k_sel)
        c_sel = jax.vmap(lambda cb, ib: cb[ib])(c_kv, idx)
        dist = t[None, :, None] - idx
        bias = jnp.moveaxis(rel_bias[t5_bucket(dist)], -1, 2)
        logits = (jnp.einsum('bqhc,bqkc->bqhk', ql, c_sel).astype(jnp.float32) * (HEAD_DIM ** -0.5)
                  + bias.astype(jnp.float32))
        logits = jnp.where((dist >= 0)[:, :, None, :], logits, NEG)
        p = jax.nn.softmax(logits, axis=-1).astype(c_kv.dtype)
        return jnp.einsum('bqhk,bqkc->bqhc', p, c_sel)

    nb = s // Q_BLOCK
    o_lat = from_blocks(lax.map(block, (jnp.arange(nb), to_blocks(q_lat), to_blocks(q_idx), to_blocks(w_idx))))
    return jnp.einsum('bshc,chd->bshd', o_lat, w_uv)


def fox_attention(q, k, v, log_f):
    b, s = q.shape[:2]
    cum = jnp.cumsum(log_f, axis=1)
    cum_k = jnp.moveaxis(cum, 1, 2)
    key_pos = jnp.arange(s)

    def block(args):
        blk, qb, cq = args
        t = blk * Q_BLOCK + jnp.arange(Q_BLOCK)
        logits = (jnp.einsum('bqhd,bshd->bhqs', qb, k).astype(jnp.float32) * (HEAD_DIM ** -0.5)
                  + jnp.moveaxis(cq, 1, 2)[..., None] - cum_k[:, :, None, :])
        logits = jnp.where(key_pos[None, None, None, :] <= t[None, None, :, None], logits, NEG)
        p = jax.nn.softmax(logits, axis=-1).astype(v.dtype)
        return jnp.einsum('bhqs,bshd->bqhd', p, v)

    nb = s // Q_BLOCK
    return from_blocks(lax.map(block, (jnp.arange(nb), to_blocks(q), to_blocks(cum))))


def token_mixing(h, w_in, b_forget, g_ckv, w_uk, w_uv, w_out, rel_bias):
    b, s, _ = h.shape
    sizes = [H_DSA * HEAD_DIM, D_LATENT, H_IDX * D_IDX, D_IDX, H_IDX, 3 * H_FOX * HEAD_DIM, H_FOX]
    points = []
    acc = 0
    for sz in sizes[:-1]:
        acc += sz
        points.append(acc)
    proj = h @ w_in
    q_a, c_raw, q_i, k_i, w_i, qkv_f, f_logit = jnp.split(proj, points, axis=-1)
    q_a = q_a.reshape(b, s, H_DSA, HEAD_DIM)
    c_kv = rms_norm(c_raw, g_ckv)
    q_i = q_i.reshape(b, s, H_IDX, D_IDX)
    qkv_f = qkv_f.reshape(b, s, 3, H_FOX, HEAD_DIM)
    log_f = jax.nn.log_sigmoid(f_logit.astype(jnp.float32) + b_forget.astype(jnp.float32))
    o_a = dsa_attention(q_a, c_kv, q_i, k_i, w_i, w_uk, w_uv, rel_bias)
    o_f = fox_attention(qkv_f[:, :, 0], qkv_f[:, :, 1], qkv_f[:, :, 2], log_f)
    o = jnp.concatenate([o_a.reshape(b, s, H_DSA * HEAD_DIM), o_f.reshape(b, s, H_FOX * HEAD_DIM)], axis=-1)
    return o @ w_out


def setup_inputs(seed: int = 0) -> dict:
    key = jax.random.key(seed)
    ks = jax.random.split(key, 20)
    in_width = H_DSA * HEAD_DIM + D_LATENT + H_IDX * D_IDX + D_IDX + H_IDX + 3 * H_FOX * HEAD_DIM + H_FOX
    f32 = jnp.float32

    def nrm(k, shape, fan_in):
        return jax.random.normal(k, shape, f32) * (fan_in ** -0.5)

    def gain(k, shape):
        return 1.0 + 0.05 * jax.random.normal(k, shape, f32)

    return {
        'x': jax.random.normal(ks[0], (BATCH, SEQ, D_MODEL), f32),
        'ffn1_norm': gain(ks[1], (DEPTH, D_MODEL)),
        'ffn1_gate': nrm(ks[2], (DEPTH, D_MODEL, D_FF), D_MODEL),
        'ffn1_up': nrm(ks[3], (DEPTH, D_MODEL, D_FF), D_MODEL),
        'ffn1_down': nrm(ks[4], (DEPTH, D_FF, D_MODEL), D_FF),
        'mix_norm': gain(ks[5], (DEPTH, D_MODEL)),
        'w_in': nrm(ks[6], (DEPTH, D_MODEL, in_width), D_MODEL),
        'b_forget': 2.0 + 3.0 * jax.random.uniform(ks[7], (DEPTH, H_FOX), f32),
        'ckv_norm': gain(ks[8], (DEPTH, D_LATENT)),
        'w_uk': nrm(ks[9], (DEPTH, D_LATENT, H_DSA, HEAD_DIM), D_LATENT),
        'w_uv': nrm(ks[10], (DEPTH, D_LATENT, H_DSA, HEAD_DIM), D_LATENT),
        'w_out': nrm(ks[11], (DEPTH, D_MODEL, D_MODEL), D_MODEL),
        'ffn2_norm': gain(ks[12], (DEPTH, D_MODEL)),
        'ffn2_gate': nrm(ks[13], (DEPTH, D_MODEL, D_FF), D_MODEL),
        'ffn2_up': nrm(ks[14], (DEPTH, D_MODEL, D_FF), D_MODEL),
        'ffn2_down': nrm(ks[15], (DEPTH, D_FF, D_MODEL), D_FF),
        'rel_bias': 0.5 * jax.random.normal(ks[16], (N_BUCKETS, H_DSA), f32),
        'final_norm': gain(ks[17], (D_MODEL,)),
    }


def reference(x, ffn1_norm, ffn1_gate, ffn1_up, ffn1_down, mix_norm, w_in, b_forget, ckv_norm,
              w_uk, w_uv, w_out, ffn2_norm, ffn2_gate, ffn2_up, ffn2_down, rel_bias, final_norm):
    h = x
    for l in range(DEPTH):
        h = h + 0.5 * swiglu(rms_norm(h, ffn1_norm[l]), ffn1_gate[l], ffn1_up[l], ffn1_down[l])
        h = h + token_mixing(rms_norm(h, mix_norm[l]), w_in[l], b_forget[l], ckv_norm[l],
                             w_uk[l], w_uv[l], w_out[l], rel_bias)
        h = h + 0.5 * swiglu(rms_norm(h, ffn2_norm[l]), ffn2_gate[l], ffn2_up[l], ffn2_down[l])
    return rms_norm(h, final_norm)
```

```python
import functools
import math

import jax
import jax.numpy as jnp
from jax import lax
from jax.experimental import pallas as pl
from jax.experimental.pallas import tpu as pltpu

HEAD_DIM = 64
H_IDX = 8
D_IDX = 64
TOPK_MAX = 256
MAX_DISTANCE = 128
EPS = 1e-6
NEG = -1e30

LANES = 128
V7X_VMEM_LIMIT = 56 * 1024 * 1024

INT_MIN = -(2 ** 31)
F32 = jnp.float32
BF16 = jnp.bfloat16
I32 = jnp.int32

_NT = (((1,), (1,)), ((), ()))


def _params(sem):
    return pltpu.CompilerParams(dimension_semantics=sem, vmem_limit_bytes=V7X_VMEM_LIMIT)


def _resident(shape):
    zeros = (0,) * len(shape)
    return pl.BlockSpec(shape, lambda *_: zeros, pipeline_mode=pl.Buffered(1))


def _rms(x, g):
    y = x * lax.rsqrt(jnp.mean(x * x, axis=-1, keepdims=True) + EPS)
    return y * g


def _ffn_kernel(x_ref, g_ref, wg_ref, wu_ref, wd_ref, fg_ref, o_ref, acc_ref, *, ff_chunk, final):
    x = x_ref[...]
    xn = _rms(x, g_ref[...]).astype(BF16)
    d_ff = wg_ref.shape[1]
    for c0 in range(0, d_ff, ff_chunk):
        c1 = min(c0 + ff_chunk, d_ff)
        g = jnp.dot(xn, wg_ref[:, c0:c1], preferred_element_type=F32)
        u = jnp.dot(xn, wu_ref[:, c0:c1], preferred_element_type=F32)
        a = (g * jax.nn.sigmoid(g) * u).astype(BF16)
        d = jnp.dot(a, wd_ref[c0:c1, :], preferred_element_type=F32)
        if c0 == 0:
            acc_ref[...] = d
        else:
            acc_ref[...] += d
    out = x + 0.5 * acc_ref[...]
    if final:
        out = _rms(out, fg_ref[...])
    o_ref[...] = out


def _ffn(h, g, wg, wu, wd, fg, *, final, tm=512, ff_chunk=256):
    t, d = h.shape
    d_ff = wg.shape[1]
    tm = min(tm, t)
    return pl.pallas_call(
        functools.partial(_ffn_kernel, ff_chunk=ff_chunk, final=final),
        grid=(t // tm,),
        in_specs=[
            pl.BlockSpec((tm, d), lambda i: (i, 0)),
            _resident((1, d)),
            _resident((d, d_ff)),
            _resident((d, d_ff)),
            _resident((d_ff, d)),
            _resident((1, d)),
        ],
        out_specs=pl.BlockSpec((tm, d), lambda i: (i, 0)),
        out_shape=jax.ShapeDtypeStruct((t, d), F32),
        scratch_shapes=[pltpu.VMEM((tm, d), F32)],
        compiler_params=_params(("parallel",)),
        name="ffn",
    )(h, g, wg, wu, wd, fg)


def _mix_in_kernel(x_ref, g_ref, w_ref, gc_ref, bdk_ref, bf_ref,
                   qlat_ref, ckv_ref, qi_ref, ki_ref, wi_ref, lf_ref, qf_ref, kf_ref, vf_ref,
                   *, w_dsa, w_idx, n_idx_heads, w_fox):
    xn = _rms(x_ref[...], g_ref[...]).astype(BF16)
    pos = [0]

    def proj(width):
        c0 = pos[0]
        pos[0] = c0 + width
        return jnp.dot(xn, w_ref[:, c0:c0 + width], preferred_element_type=F32)

    qk_scale = HEAD_DIM ** -0.5
    idx_scale = D_IDX ** -0.5
    qa = proj(w_dsa).astype(BF16)
    qlat = jnp.dot(qa, bdk_ref[...], preferred_element_type=F32)
    qlat_ref[...] = (qlat * qk_scale).astype(BF16)
    ckv_ref[...] = _rms(proj(LANES), gc_ref[...]).astype(BF16)
    qi_ref[...] = (proj(w_idx) * idx_scale).astype(BF16)
    ki_ref[...] = proj(LANES).astype(BF16)
    wi_ref[...] = proj(LANES)[:, :n_idx_heads] * (n_idx_heads ** -0.5)
    n_fox = bf_ref.shape[1]
    f = proj(LANES)[:, :n_fox] + bf_ref[...]
    lf_ref[...] = jnp.minimum(f, 0.0) - jnp.log1p(jnp.exp(-jnp.abs(f)))
    for hp in range(w_fox // LANES):
        qf_ref[hp] = (proj(LANES) * qk_scale).astype(BF16)
    for hp in range(w_fox // LANES):
        kf_ref[hp] = proj(LANES).astype(BF16)
    for hp in range(w_fox // LANES):
        vf_ref[hp] = proj(LANES).astype(BF16)


def _mix_in(h, g, w2, gc, bdk, bfg, *, w_dsa, w_idx, w_fox, d_lat, tm=512):
    t, d = h.shape
    tm = min(tm, t)
    n_dsa = w_dsa // HEAD_DIM
    n_fox = bfg.shape[1]
    npair = w_fox // LANES
    row = lambda i: (i, 0)
    row3 = lambda i: (0, i, 0)
    out_shape = [
        jax.ShapeDtypeStruct((t, n_dsa * d_lat), BF16),
        jax.ShapeDtypeStruct((t, d_lat), BF16),
        jax.ShapeDtypeStruct((t, w_idx), BF16),
        jax.ShapeDtypeStruct((t, LANES), BF16),
        jax.ShapeDtypeStruct((t, H_IDX), F32),
        jax.ShapeDtypeStruct((t, n_fox), F32),
        jax.ShapeDtypeStruct((npair, t, LANES), BF16),
        jax.ShapeDtypeStruct((npair, t, LANES), BF16),
        jax.ShapeDtypeStruct((npair, t, LANES), BF16),
    ]
    out_specs = [
        pl.BlockSpec((tm, n_dsa * d_lat), row),
        pl.BlockSpec((tm, d_lat), row),
        pl.BlockSpec((tm, w_idx), row),
        pl.BlockSpec((tm, LANES), row),
        pl.BlockSpec((tm, H_IDX), row),
        pl.BlockSpec((tm, n_fox), row),
        pl.BlockSpec((npair, tm, LANES), row3),
        pl.BlockSpec((npair, tm, LANES), row3),
        pl.BlockSpec((npair, tm, LANES), row3),
    ]
    return pl.pallas_call(
        functools.partial(_mix_in_kernel, w_dsa=w_dsa, w_idx=w_idx, n_idx_heads=H_IDX, w_fox=w_fox),
        grid=(t // tm,),
        in_specs=[
            pl.BlockSpec((tm, d), row),
            _resident((1, d)),
            _resident(w2.shape),
            _resident((1, d_lat)),
            _resident(bdk.shape),
            _resident((1, n_fox)),
        ],
        out_specs=out_specs,
        out_shape=out_shape,
        compiler_params=_params(("parallel",)),
        name="mix_in",
    )(h, g, w2, gc, bdk, bfg)


def _cumsum_kernel(x_ref, o_ref):
    x = x_ref[...]
    n = x.shape[1]
    lane = lax.broadcasted_iota(I32, x.shape, 1)
    sh = 1
    while sh < n:
        x = x + jnp.where(lane >= sh, pltpu.roll(x, sh, axis=1), 0.0)
        sh *= 2
    o_ref[...] = x


def _cumsum_rows(x):
    r, n = x.shape
    rb = 8 if r % 8 == 0 else r
    return pl.pallas_call(
        _cumsum_kernel,
        grid=(r // rb,),
        in_specs=[pl.BlockSpec((rb, n), lambda i: (i, 0))],
        out_specs=pl.BlockSpec((rb, n), lambda i: (i, 0)),
        out_shape=jax.ShapeDtypeStruct((r, n), F32),
        compiler_params=_params(("parallel",)),
        name="cumsum",
    )(x)


def _bias_tiles_kernel(rb_ref, bucket_ref, o_ref):
    n_buckets, n_heads = rb_ref.shape
    for j in range(bucket_ref.shape[0]):
        bucket = bucket_ref[j]
        for h in range(n_heads):
            acc = jnp.zeros(bucket.shape, F32)
            for b in range(n_buckets):
                acc = jnp.where(bucket == b, rb_ref[b, h], acc)
            o_ref[j, h] = acc


def _t5_bucket(dist, n_buckets):
    n = jnp.maximum(dist, 0)
    max_exact = n_buckets // 2
    nf = jnp.maximum(n, 1).astype(F32)
    large = max_exact + (jnp.log(nf / max_exact) / math.log(MAX_DISTANCE / max_exact)
                         * (n_buckets - max_exact)).astype(I32)
    large = jnp.minimum(large, n_buckets - 1)
    return jnp.where(n < max_exact, n, large)


def _bias_tiles(rel_bias, tile):
    n_buckets, n_heads = rel_bias.shape
    assert tile >= MAX_DISTANCE
    r = lax.broadcasted_iota(I32, (tile, tile), 0)
    c = lax.broadcasted_iota(I32, (tile, tile), 1)
    buckets = jnp.stack([_t5_bucket(j * tile + r - c, n_buckets) for j in range(2)])
    return pl.pallas_call(
        _bias_tiles_kernel,
        in_specs=[pl.BlockSpec(memory_space=pltpu.SMEM), pl.BlockSpec(memory_space=pltpu.VMEM)],
        out_specs=pl.BlockSpec(memory_space=pltpu.VMEM),
        out_shape=jax.ShapeDtypeStruct((2, n_heads, tile, tile), F32),
        compiler_params=pltpu.CompilerParams(vmem_limit_bytes=V7X_VMEM_LIMIT),
        name="bias_tiles",
    )(rel_bias, buckets)


def _online_softmax_step(s, v, m_ref, l_ref, acc_ref, h):
    m_old = m_ref[h]
    m_new = jnp.maximum(m_old, jnp.max(s, axis=1, keepdims=True))
    alpha = jnp.exp(m_old - m_new)
    p = jnp.exp(s - m_new)
    l_ref[h] = alpha * l_ref[h] + jnp.sum(p, axis=1, keepdims=True)
    acc_ref[h] = alpha * acc_ref[h] + jnp.dot(p.astype(BF16), v, preferred_element_type=F32)
    m_ref[h] = m_new


def _dsa_kernel(rb_ref, qlat_ref, qi_ref, wi_ref, ki_ref, ckv_ref, bias_ref, o_ref,
                keys_ref, m_ref, l_ref, acc_ref, *, tile, k_sel, n_heads, d_lat, seq):
    i = pl.program_id(1)
    n_chunks = i + 1
    row = lax.broadcasted_iota(I32, (tile, tile), 0)
    col = lax.broadcasted_iota(I32, (tile, tile), 1)
    diff = col - row
    lane = lax.broadcasted_iota(I32, (tile, LANES), 1)
    low_half = lane < D_IDX

    def chunk_start(kc):
        return pl.multiple_of(kc * tile, tile)

    qi = qi_ref[...]
    wi = wi_ref[...]
    zero = jnp.zeros((), BF16)
    qm = []
    for h in range(H_IDX):
        grp = qi[:, (h // 2) * LANES:(h // 2 + 1) * LANES]
        qm.append(jnp.where(low_half, grp, zero) if h % 2 == 0 else jnp.where(low_half, zero, grp))

    def score_chunk(kc, causal):
        off = chunk_start(kc)
        k2 = ki_ref[pl.ds(off, tile), :]
        acc = None
        for h in range(H_IDX):
            r = lax.dot_general(qm[h], k2, _NT, preferred_element_type=F32)
            term = jnp.maximum(r, 0.0) * wi[:, h:h + 1]
            acc = term if acc is None else acc + term
        if causal:
            acc = jnp.where(diff <= 0, acc, NEG)
        bits = pltpu.bitcast(acc, I32)
        keys_ref[:, pl.ds(off, tile)] = jnp.where(bits < 0, INT_MIN - bits, bits)

    def score_body(kc, carry):
        score_chunk(kc, False)
        return carry

    lax.fori_loop(0, i, score_body, 0)
    score_chunk(i, True)

    def count(pred):
        def body(c, cnt):
            off = chunk_start(c)
            m = pred(keys_ref[:, pl.ds(off, tile)], c).astype(I32)
            part = m[:, 0:LANES]
            for j in range(1, tile // LANES):
                part = part + m[:, j * LANES:(j + 1) * LANES]
            return cnt + part
        cnt = lax.fori_loop(0, n_chunks, body, jnp.zeros((tile, LANES), I32))
        return jnp.sum(cnt, axis=1, keepdims=True)

    c_nonneg = count(lambda k, c: k >= 0)
    prefix = jnp.where(c_nonneg >= k_sel, 0, INT_MIN).astype(I32)

    def bit_body(it, prefix):
        cand = prefix | lax.shift_left(jnp.int32(1), 30 - it)
        cnt = count(lambda k, c: k >= cand)
        return jnp.where(cnt >= k_sel, cand, prefix)

    thr = lax.fori_loop(0, 31, bit_body, prefix)

    n_ge = count(lambda k, c: k >= thr)
    n_gt = count(lambda k, c: k > thr)
    need = k_sel - n_gt

    @pl.when(jnp.max(n_ge) > k_sel)
    def _():
        pos_bits = max(seq - 1, 1).bit_length()

        def pos_body(it, x):
            cand = x | lax.shift_left(jnp.int32(1), pos_bits - 1 - it)
            f = count(lambda k, c: (k == thr) & (col + c * tile < cand))
            return jnp.where(f < need, cand, x)

        last = lax.fori_loop(0, pos_bits, pos_body, jnp.zeros((tile, 1), I32))

        def drop_body(c, carry):
            off = chunk_start(c)
            k = keys_ref[:, pl.ds(off, tile)]
            drop = (k == thr) & (col + c * tile > last)
            keys_ref[:, pl.ds(off, tile)] = jnp.where(drop, INT_MIN, k)
            return carry

        lax.fori_loop(0, n_chunks, drop_body, 0)

    m_ref[...] = jnp.full(m_ref.shape, NEG, F32)
    l_ref[...] = jnp.zeros(l_ref.shape, F32)
    acc_ref[...] = jnp.zeros(acc_ref.shape, F32)
    far_bucket = rb_ref.shape[0] - 1

    def attn_chunk(kc, near):
        off = chunk_start(kc)
        sel = keys_ref[:, pl.ds(off, tile)] >= thr
        if near:
            j = i - kc
            sel = sel & (diff <= j * tile)
        c = ckv_ref[pl.ds(off, tile), :]
        for h in range(n_heads):
            q = qlat_ref[:, h * d_lat:(h + 1) * d_lat]
            s = lax.dot_general(q, c, _NT, preferred_element_type=F32)
            s = s + (bias_ref[j, h] if near else rb_ref[far_bucket, h])
            s = jnp.where(sel, s, NEG)
            _online_softmax_step(s, c, m_ref, l_ref, acc_ref, h)

    def far_body(kc, carry):
        attn_chunk(kc, False)
        return carry

    def near_body(kc, carry):
        attn_chunk(kc, True)
        return carry

    first_near = jnp.maximum(i - 1, 0)
    lax.fori_loop(0, first_near, far_body, 0)
    lax.fori_loop(first_near, i + 1, near_body, 0)

    for h in range(n_heads):
        o_ref[:, h * d_lat:(h + 1) * d_lat] = (acc_ref[h] / l_ref[h]).astype(BF16)


def _dsa(rel_bias, qlat, qi, wi, ki2, ckv, bias_tiles, *, k_sel, tile):
    b, s, w_lat = qlat.shape
    d_lat = ckv.shape[2]
    n_heads = w_lat // d_lat
    w_idx = qi.shape[2]
    qblk = lambda bi, i: (bi, i, 0)
    whole = lambda bi, i: (bi, 0, 0)
    return pl.pallas_call(
        functools.partial(_dsa_kernel, tile=tile, k_sel=k_sel, n_heads=n_heads, d_lat=d_lat, seq=s),
        grid=(b, s // tile),
        in_specs=[
            pl.BlockSpec(memory_space=pltpu.SMEM),
            pl.BlockSpec((None, tile, w_lat), qblk),
            pl.BlockSpec((None, tile, w_idx), qblk),
            pl.BlockSpec((None, tile, H_IDX), qblk),
            pl.BlockSpec((None, s, LANES), whole),
            pl.BlockSpec((None, s, d_lat), whole),
            _resident(bias_tiles.shape),
        ],
        out_specs=pl.BlockSpec((None, tile, w_lat), qblk),
        out_shape=jax.ShapeDtypeStruct((b, s, w_lat), BF16),
        scratch_shapes=[
            pltpu.VMEM((tile, s), I32),
            pltpu.VMEM((n_heads, tile, 1), F32),
            pltpu.VMEM((n_heads, tile, 1), F32),
            pltpu.VMEM((n_heads, tile, d_lat), F32),
        ],
        compiler_params=_params(("parallel", "arbitrary")),
        name="dsa",
    )(rel_bias, qlat, qi, wi, ki2, ckv, bias_tiles)


def _fox_kernel(q_ref, k_ref, v_ref, cq_ref, ck_ref, o_ref, m_ref, l_ref, acc_ref, *, tile):
    i = pl.program_id(2)
    row = lax.broadcasted_iota(I32, (tile, tile), 0)
    col = lax.broadcasted_iota(I32, (tile, tile), 1)
    causal = col <= row
    lane = lax.broadcasted_iota(I32, (tile, LANES), 1)
    low_half = lane < HEAD_DIM
    q = q_ref[...]
    zero = jnp.zeros((), BF16)
    qh = [jnp.where(low_half, q, zero), jnp.where(low_half, zero, q)]
    cq = cq_ref[...]
    m_ref[...] = jnp.full(m_ref.shape, NEG, F32)
    l_ref[...] = jnp.zeros(l_ref.shape, F32)
    acc_ref[...] = jnp.zeros(acc_ref.shape, F32)

    def chunk(kc, diagonal):
        off = pl.multiple_of(kc * tile, tile)
        k = k_ref[pl.ds(off, tile), :]
        v = v_ref[pl.ds(off, tile), :]
        ck = ck_ref[:, pl.ds(off, tile)]
        for h in range(2):
            s = lax.dot_general(qh[h], k, _NT, preferred_element_type=F32)
            s = s + cq[:, h:h + 1] - ck[h:h + 1, :]
            if diagonal:
                s = jnp.where(causal, s, NEG)
            _online_softmax_step(s, v, m_ref, l_ref, acc_ref, h)

    def body(kc, carry):
        chunk(kc, False)
        return carry

    lax.fori_loop(0, i, body, 0)
    chunk(i, True)
    o0 = acc_ref[0] / l_ref[0]
    o1 = acc_ref[1] / l_ref[1]
    o_ref[...] = jnp.where(low_half, o0, o1).astype(BF16)


def _fox(qf, kf, vf, cum_col, cum_row, *, tile):
    npair, b, s, _ = qf.shape
    qblk = lambda bi, hp, i: (hp, bi, i, 0)
    whole = lambda bi, hp, i: (hp, bi, 0, 0)
    return pl.pallas_call(
        functools.partial(_fox_kernel, tile=tile),
        grid=(b, npair, s // tile),
        in_specs=[
            pl.BlockSpec((None, None, tile, LANES), qblk),
            pl.BlockSpec((None, None, s, LANES), whole),
            pl.BlockSpec((None, None, s, LANES), whole),
            pl.BlockSpec((None, None, tile, 2), lambda bi, hp, i: (bi, hp, i, 0)),
            pl.BlockSpec((None, None, 2, s), lambda bi, hp, i: (bi, hp, 0, 0)),
        ],
        out_specs=pl.BlockSpec((None, None, tile, LANES), qblk),
        out_shape=jax.ShapeDtypeStruct((npair, b, s, LANES), BF16),
        scratch_shapes=[
            pltpu.VMEM((2, tile, 1), F32),
            pltpu.VMEM((2, tile, 1), F32),
            pltpu.VMEM((2, tile, LANES), F32),
        ],
        compiler_params=_params(("parallel", "parallel", "arbitrary")),
        name="fox",
    )(qf, kf, vf, cum_col, cum_row)


def _mix_out_kernel(h_ref, ol_ref, of_ref, bdv_ref, wo_ref, o_ref):
    oa = jnp.dot(ol_ref[...], bdv_ref[...], preferred_element_type=F32).astype(BF16)
    w_dsa = oa.shape[1]
    y = jnp.dot(oa, wo_ref[0:w_dsa, :], preferred_element_type=F32)
    for hp in range(of_ref.shape[0]):
        r0 = w_dsa + hp * LANES
        y = y + jnp.dot(of_ref[hp], wo_ref[r0:r0 + LANES, :], preferred_element_type=F32)
    o_ref[...] = h_ref[...] + y


def _mix_out(h, olat, of, bdv, wo, *, tm=512):
    t, d = h.shape
    tm = min(tm, t)
    npair = of.shape[0]
    row = lambda i: (i, 0)
    return pl.pallas_call(
        _mix_out_kernel,
        grid=(t // tm,),
        in_specs=[
            pl.BlockSpec((tm, d), row),
            pl.BlockSpec((tm, olat.shape[1]), row),
            pl.BlockSpec((npair, tm, LANES), lambda i: (0, i, 0)),
            _resident(bdv.shape),
            _resident(wo.shape),
        ],
        out_specs=pl.BlockSpec((tm, d), row),
        out_shape=jax.ShapeDtypeStruct((t, d), F32),
        compiler_params=_params(("parallel",)),
        name="mix_out",
    )(h, olat, of, bdv, wo)


def _pad_cols(w, width):
    return jnp.pad(w, ((0, 0), (0, width - w.shape[1])))


def _mixer_weights(w_in, w_uk, w_uv, n_fox):
    d_lat, n_dsa, hd = w_uk.shape
    w_dsa = n_dsa * hd
    w_idx = H_IDX * D_IDX
    w_fox = n_fox * hd
    sizes = [w_dsa, d_lat, w_idx, D_IDX, H_IDX, 3 * w_fox, n_fox]
    assert sum(sizes) == w_in.shape[1] and d_lat == LANES and 2 * D_IDX == LANES
    offs = [0]
    for sz in sizes:
        offs.append(offs[-1] + sz)
    q_a, c_raw, q_i, k_i, w_i, qkv_f, f_l = [w_in[:, offs[j]:offs[j + 1]] for j in range(7)]
    w2 = jnp.concatenate(
        [q_a, c_raw, q_i, k_i, k_i, _pad_cols(w_i, LANES), _pad_cols(f_l, LANES), qkv_f], axis=1).astype(BF16)
    eye = jnp.eye(n_dsa, dtype=w_uk.dtype)
    bdk = jnp.einsum('chd,hg->hdgc', w_uk, eye).reshape(w_dsa, n_dsa * d_lat).astype(BF16)
    bdv = jnp.einsum('chd,hg->hcgd', w_uv, eye).reshape(n_dsa * d_lat, w_dsa).astype(BF16)
    return w2, bdk, bdv, dict(w_dsa=w_dsa, w_idx=w_idx, w_fox=w_fox, d_lat=d_lat)


def kernel(x, ffn1_norm, ffn1_gate, ffn1_up, ffn1_down, mix_norm, w_in, b_forget, ckv_norm, w_uk, w_uv, w_out,
           ffn2_norm, ffn2_gate, ffn2_up, ffn2_down, rel_bias, final_norm):
    b, s, d = x.shape
    depth = w_in.shape[0]
    n_fox = b_forget.shape[1]
    t = b * s
    tile = min(256, s)
    k_sel = min(TOPK_MAX, s // 4)
    npair = n_fox * HEAD_DIM // LANES

    bias_tiles = _bias_tiles(rel_bias, tile)
    fg = final_norm.reshape(1, d)
    h = x.reshape(t, d)
    for l in range(depth):
        h = _ffn(h, ffn1_norm[l].reshape(1, d), ffn1_gate[l].astype(BF16), ffn1_up[l].astype(BF16),
                 ffn1_down[l].astype(BF16), fg, final=False)

        w2, bdk, bdv, dims = _mixer_weights(w_in[l], w_uk[l], w_uv[l], n_fox)
        qlat, ckv, qi, ki2, wi, logf, qf, kf, vf = _mix_in(
            h, mix_norm[l].reshape(1, d), w2, ckv_norm[l].reshape(1, -1), bdk, b_forget[l].reshape(1, n_fox), **dims)

        cum_row = _cumsum_rows(jnp.swapaxes(logf.reshape(b, s, n_fox), 1, 2).reshape(b * n_fox, s))
        cum_row = cum_row.reshape(b, npair, 2, s)
        cum_col = jnp.swapaxes(cum_row, 2, 3)

        olat = _dsa(rel_bias, qlat.reshape(b, s, -1), qi.reshape(b, s, -1), wi.reshape(b, s, -1),
                    ki2.reshape(b, s, -1), ckv.reshape(b, s, -1), bias_tiles, k_sel=k_sel, tile=tile)
        of = _fox(qf.reshape(npair, b, s, LANES), kf.reshape(npair, b, s, LANES), vf.reshape(npair, b, s, LANES),
                  cum_col, cum_row, tile=tile)

        h = _mix_out(h, olat.reshape(t, -1), of.reshape(npair, t, LANES), bdv, w_out[l].astype(BF16))
        h = _ffn(h, ffn2_norm[l].reshape(1, d), ffn2_gate[l].astype(BF16), ffn2_up[l].astype(BF16),
                 ffn2_down[l].astype(BF16), fg, final=(l == depth - 1))
    return h.reshape(b, s, d)
```

```python
import functools
import math

import jax
import jax.numpy as jnp
from jax import lax
from jax.experimental import pallas as pl
from jax.experimental.pallas import tpu as pltpu

HEAD_DIM = 64
H_IDX = 8
D_IDX = 64
TOPK_MAX = 256
MAX_DISTANCE = 128
EPS = 1e-6
NEG = -1e30

LANES = 128
V7X_VMEM_LIMIT = 56 * 1024 * 1024

DSA_TILE = 256
DSA_SEARCH_ROWS = 128
FOX_TILE = 256
ATTN_STRIP = 32

INT_MIN = -(2 ** 31)
F32 = jnp.float32
BF16 = jnp.bfloat16
I32 = jnp.int32

_NT = (((1,), (1,)), ((), ()))


def _params(sem):
    return pltpu.CompilerParams(dimension_semantics=sem, vmem_limit_bytes=V7X_VMEM_LIMIT)


def _resident(shape):
    zeros = (0,) * len(shape)
    return pl.BlockSpec(shape, lambda *_: zeros, pipeline_mode=pl.Buffered(1))


def _rms(x, g):
    y = x * lax.rsqrt(jnp.mean(x * x, axis=-1, keepdims=True) + EPS)
    return y * g


def _ffn_kernel(x_ref, g_ref, wg_ref, wu_ref, wd_ref, fg_ref, o_ref, acc_ref, *, ff_chunk, final):
    x = x_ref[...]
    xn = _rms(x, g_ref[...]).astype(BF16)
    d_ff = wg_ref.shape[1]
    for c0 in range(0, d_ff, ff_chunk):
        c1 = min(c0 + ff_chunk, d_ff)
        g = jnp.dot(xn, wg_ref[:, c0:c1], preferred_element_type=F32)
        u = jnp.dot(xn, wu_ref[:, c0:c1], preferred_element_type=F32)
        a = (g * jax.nn.sigmoid(g) * u).astype(BF16)
        d = jnp.dot(a, wd_ref[c0:c1, :], preferred_element_type=F32)
        if c0 == 0:
            acc_ref[...] = d
        else:
            acc_ref[...] += d
    out = x + 0.5 * acc_ref[...]
    if final:
        out = _rms(out, fg_ref[...])
    o_ref[...] = out


def _ffn(h, g, wg, wu, wd, fg, *, final, tm=512, ff_chunk=256):
    t, d = h.shape
    d_ff = wg.shape[1]
    tm = min(tm, t)
    return pl.pallas_call(
        functools.partial(_ffn_kernel, ff_chunk=ff_chunk, final=final),
        grid=(t // tm,),
        in_specs=[
            pl.BlockSpec((tm, d), lambda i: (i, 0)),
            _resident((1, d)),
            _resident((d, d_ff)),
            _resident((d, d_ff)),
            _resident((d_ff, d)),
            _resident((1, d)),
        ],
        out_specs=pl.BlockSpec((tm, d), lambda i: (i, 0)),
        out_shape=jax.ShapeDtypeStruct((t, d), F32),
        scratch_shapes=[pltpu.VMEM((tm, d), F32)],
        compiler_params=_params(("parallel",)),
        name="ffn",
    )(h, g, wg, wu, wd, fg)


def _mix_in_kernel(x_ref, g_ref, w_ref, gc_ref, bdk_ref, bf_ref,
                   qlat_ref, ckv_ref, qi_ref, ki_ref, wi_ref, lf_ref, qf_ref, kf_ref, vf_ref,
                   *, w_dsa, w_idx, n_idx_heads, w_fox):
    xn = _rms(x_ref[...], g_ref[...]).astype(BF16)
    pos = [0]

    def proj(width):
        c0 = pos[0]
        pos[0] = c0 + width
        return jnp.dot(xn, w_ref[:, c0:c0 + width], preferred_element_type=F32)

    qk_scale = HEAD_DIM ** -0.5
    idx_scale = D_IDX ** -0.5
    qa = proj(w_dsa).astype(BF16)
    qlat = jnp.dot(qa, bdk_ref[...], preferred_element_type=F32)
    for h in range(qlat_ref.shape[0]):
        qlat_ref[h] = (qlat[:, h * LANES:(h + 1) * LANES] * qk_scale).astype(BF16)
    ckv_ref[...] = _rms(proj(LANES), gc_ref[...]).astype(BF16)
    qi_ref[...] = (proj(w_idx) * idx_scale).astype(BF16)
    ki_ref[...] = proj(LANES).astype(BF16)
    wi_ref[...] = proj(LANES)[:, :n_idx_heads] * (n_idx_heads ** -0.5)
    n_fox = bf_ref.shape[1]
    f = proj(LANES)[:, :n_fox] + bf_ref[...]
    lf_ref[...] = jnp.minimum(f, 0.0) - jnp.log1p(jnp.exp(-jnp.abs(f)))
    for hp in range(w_fox // LANES):
        qf_ref[hp] = (proj(LANES) * qk_scale).astype(BF16)
    for hp in range(w_fox // LANES):
        kf_ref[hp] = proj(LANES).astype(BF16)
    for hp in range(w_fox // LANES):
        vf_ref[hp] = proj(LANES).astype(BF16)


def _mix_in(h, g, w2, gc, bdk, bfg, *, w_dsa, w_idx, w_fox, d_lat, tm=512):
    t, d = h.shape
    tm = min(tm, t)
    n_dsa = w_dsa // HEAD_DIM
    n_fox = bfg.shape[1]
    npair = w_fox // LANES
    row = lambda i: (i, 0)
    row3 = lambda i: (0, i, 0)
    out_shape = [
        jax.ShapeDtypeStruct((n_dsa, t, d_lat), BF16),
        jax.ShapeDtypeStruct((t, d_lat), BF16),
        jax.ShapeDtypeStruct((t, w_idx), BF16),
        jax.ShapeDtypeStruct((t, LANES), BF16),
        jax.ShapeDtypeStruct((t, H_IDX), F32),
        jax.ShapeDtypeStruct((t, n_fox), F32),
        jax.ShapeDtypeStruct((npair, t, LANES), BF16),
        jax.ShapeDtypeStruct((npair, t, LANES), BF16),
        jax.ShapeDtypeStruct((npair, t, LANES), BF16),
    ]
    out_specs = [
        pl.BlockSpec((n_dsa, tm, d_lat), row3),
        pl.BlockSpec((tm, d_lat), row),
        pl.BlockSpec((tm, w_idx), row),
        pl.BlockSpec((tm, LANES), row),
        pl.BlockSpec((tm, H_IDX), row),
        pl.BlockSpec((tm, n_fox), row),
        pl.BlockSpec((npair, tm, LANES), row3),
        pl.BlockSpec((npair, tm, LANES), row3),
        pl.BlockSpec((npair, tm, LANES), row3),
    ]
    return pl.pallas_call(
        functools.partial(_mix_in_kernel, w_dsa=w_dsa, w_idx=w_idx, n_idx_heads=H_IDX, w_fox=w_fox),
        grid=(t // tm,),
        in_specs=[
            pl.BlockSpec((tm, d), row),
            _resident((1, d)),
            _resident(w2.shape),
            _resident((1, d_lat)),
            _resident(bdk.shape),
            _resident((1, n_fox)),
        ],
        out_specs=out_specs,
        out_shape=out_shape,
        compiler_params=_params(("parallel",)),
        name="mix_in",
    )(h, g, w2, gc, bdk, bfg)


def _cumsum_kernel(x_ref, o_ref):
    x = x_ref[...]
    n = x.shape[1]
    lane = lax.broadcasted_iota(I32, x.shape, 1)
    sh = 1
    while sh < n:
        x = x + jnp.where(lane >= sh, pltpu.roll(x, sh, axis=1), 0.0)
        sh *= 2
    o_ref[...] = x


def _cumsum_rows(x):
    r, n = x.shape
    rb = 8 if r % 8 == 0 else r
    return pl.pallas_call(
        _cumsum_kernel,
        grid=(r // rb,),
        in_specs=[pl.BlockSpec((rb, n), lambda i: (i, 0))],
        out_specs=pl.BlockSpec((rb, n), lambda i: (i, 0)),
        out_shape=jax.ShapeDtypeStruct((r, n), F32),
        compiler_params=_params(("parallel",)),
        name="cumsum",
    )(x)


def _bias_tiles_kernel(rb_ref, bucket_ref, o_ref):
    n_buckets, n_heads = rb_ref.shape
    for j in range(bucket_ref.shape[0]):
        bucket = bucket_ref[j]
        for h in range(n_heads):
            acc = jnp.zeros(bucket.shape, F32)
            for b in range(n_buckets):
                acc = jnp.where(bucket == b, rb_ref[b, h], acc)
            o_ref[j, h] = acc - rb_ref[n_buckets - 1, h]


def _t5_bucket(dist, n_buckets):
    n = jnp.maximum(dist, 0)
    max_exact = n_buckets // 2
    nf = jnp.maximum(n, 1).astype(F32)
    large = max_exact + (jnp.log(nf / max_exact) / math.log(MAX_DISTANCE / max_exact)
                         * (n_buckets - max_exact)).astype(I32)
    large = jnp.minimum(large, n_buckets - 1)
    return jnp.where(n < max_exact, n, large)


def _bias_tiles(rel_bias, tile):
    n_buckets, n_heads = rel_bias.shape
    assert tile >= MAX_DISTANCE
    r = lax.broadcasted_iota(I32, (tile, tile), 0)
    c = lax.broadcasted_iota(I32, (tile, tile), 1)
    buckets = jnp.stack([_t5_bucket(j * tile + r - c, n_buckets) for j in range(2)])
    return pl.pallas_call(
        _bias_tiles_kernel,
        in_specs=[pl.BlockSpec(memory_space=pltpu.SMEM), pl.BlockSpec(memory_space=pltpu.VMEM)],
        out_specs=pl.BlockSpec(memory_space=pltpu.VMEM),
        out_shape=jax.ShapeDtypeStruct((2, n_heads, tile, tile), F32),
        compiler_params=pltpu.CompilerParams(vmem_limit_bytes=V7X_VMEM_LIMIT),
        name="bias_tiles",
    )(rel_bias, buckets)


def _softmax_strip(s, rows, p_ref, alpha_ref, m_ref, l_ref):
    reps = s.shape[1] // LANES
    m_old = m_ref[rows, :]
    m_new = jnp.maximum(m_old, jnp.max(s, axis=1, keepdims=True))
    alpha = jnp.exp(m_old - m_new)
    p = jnp.exp(s - jnp.tile(m_new, (1, reps)))
    psum = p[:, 0:LANES]
    for j in range(1, reps):
        psum = psum + p[:, j * LANES:(j + 1) * LANES]
    l_ref[rows, :] = alpha * l_ref[rows, :] + psum
    alpha_ref[rows, :] = alpha
    m_ref[rows, :] = m_new
    p_ref[rows, :] = p.astype(BF16)


def _dsa_kernel(qlat_ref, qi_ref, wi_ref, ki_ref, ckv_ref, bias_ref, o_ref,
                keys_ref, thr_ref, s_ref, p_ref, alpha_ref, m_ref, l_ref, acc_ref,
                *, tile, k_sel, seq, search_rows, strip):
    n_heads, _, d_lat = qlat_ref.shape
    i = pl.program_id(1)
    n_chunks = i + 1
    row = lax.broadcasted_iota(I32, (tile, tile), 0)
    col = lax.broadcasted_iota(I32, (tile, tile), 1)
    diff = col - row
    lane = lax.broadcasted_iota(I32, (tile, LANES), 1)
    low_half = lane < D_IDX

    def chunk_start(kc):
        return pl.multiple_of(kc * tile, tile)

    qi = qi_ref[...]
    wi = wi_ref[...]
    zero = jnp.zeros((), BF16)
    qm = []
    for h in range(H_IDX):
        grp = qi[:, (h // 2) * LANES:(h // 2 + 1) * LANES]
        qm.append(jnp.where(low_half, grp, zero) if h % 2 == 0 else jnp.where(low_half, zero, grp))

    def score_chunk(kc, causal):
        off = chunk_start(kc)
        k2 = ki_ref[pl.ds(off, tile), :]
        acc = None
        for h in range(H_IDX):
            r = lax.dot_general(qm[h], k2, _NT, preferred_element_type=F32)
            term = jnp.maximum(r, 0.0) * wi[:, h:h + 1]
            acc = term if acc is None else acc + term
        if causal:
            acc = jnp.where(diff <= 0, acc, NEG)
        bits = pltpu.bitcast(acc, I32)
        keys_ref[:, pl.ds(off, tile)] = jnp.where(bits < 0, INT_MIN - bits, bits)

    def score_body(kc, carry):
        score_chunk(kc, False)
        return carry

    lax.fori_loop(0, i, score_body, 0)
    score_chunk(i, True)

    bcol = lax.broadcasted_iota(I32, (search_rows, tile), 1)

    def search_block(rb, carry):
        rows = pl.ds(pl.multiple_of(rb * search_rows, search_rows), search_rows)

        def count(pred):
            def body(c, cnt):
                m = pred(keys_ref[rows, pl.ds(chunk_start(c), tile)], c).astype(I32)
                part = m[:, 0:LANES]
                for j in range(1, tile // LANES):
                    part = part + m[:, j * LANES:(j + 1) * LANES]
                return cnt + part
            cnt = lax.fori_loop(0, n_chunks, body, jnp.zeros((search_rows, LANES), I32))
            return jnp.sum(cnt, axis=1, keepdims=True)

        c_nonneg = count(lambda k, c: k >= 0)
        prefix = jnp.where(c_nonneg >= k_sel, 0, INT_MIN).astype(I32)

        def bit_body(it, prefix):
            cand = prefix | lax.shift_left(jnp.int32(1), 30 - it)
            cnt = count(lambda k, c: k >= cand)
            return jnp.where(cnt >= k_sel, cand, prefix)

        thr = lax.fori_loop(0, 31, bit_body, prefix)
        thr_ref[rows, :] = jnp.broadcast_to(thr, (search_rows, LANES))

        n_ge = count(lambda k, c: k >= thr)
        n_gt = count(lambda k, c: k > thr)
        need = k_sel - n_gt

        @pl.when(jnp.max(n_ge) > k_sel)
        def _():
            pos_bits = max(seq - 1, 1).bit_length()

            def pos_body(it, x):
                cand = x | lax.shift_left(jnp.int32(1), pos_bits - 1 - it)
                f = count(lambda k, c: (k == thr) & (bcol + c * tile < cand))
                return jnp.where(f < need, cand, x)

            last = lax.fori_loop(0, pos_bits, pos_body, jnp.zeros((search_rows, 1), I32))

            def drop_body(c, carry):
                cols = pl.ds(chunk_start(c), tile)
                k = keys_ref[rows, cols]
                drop = (k == thr) & (bcol + c * tile > last)
                keys_ref[rows, cols] = jnp.where(drop, INT_MIN, k)
                return carry

            lax.fori_loop(0, n_chunks, drop_body, 0)

        return carry

    lax.fori_loop(0, tile // search_rows, search_block, 0)

    m_ref[...] = jnp.full(m_ref.shape, NEG, F32)
    l_ref[...] = jnp.zeros(l_ref.shape, F32)
    acc_ref[...] = jnp.zeros(acc_ref.shape, F32)
    q_all = qlat_ref[...].reshape(n_heads * tile, d_lat)
    sdiff = diff[0:strip, :]

    def attn_chunk(kc, near):
        cols = pl.ds(chunk_start(kc), tile)
        c = ckv_ref[cols, :]
        s_ref[...] = lax.dot_general(q_all, c, _NT, preferred_element_type=F32)
        j = i - kc

        for r0 in range(0, tile, strip):
            qrows = pl.ds(r0, strip)
            sel = keys_ref[qrows, cols] >= jnp.tile(thr_ref[qrows, :], (1, tile // LANES))
            if near:
                sel = sel & (sdiff - r0 <= j * tile)
            for h in range(n_heads):
                rows = pl.ds(h * tile + r0, strip)
                s = s_ref[rows, :]
                if near:
                    s = s + bias_ref[j, h, qrows, :]
                _softmax_strip(jnp.where(sel, s, NEG), rows, p_ref, alpha_ref, m_ref, l_ref)
        acc_ref[...] = alpha_ref[...] * acc_ref[...] + jnp.dot(p_ref[...], c, preferred_element_type=F32)

    def far_body(kc, carry):
        attn_chunk(kc, False)
        return carry

    def near_body(kc, carry):
        attn_chunk(kc, True)
        return carry

    first_near = jnp.maximum(i - 1, 0)
    lax.fori_loop(0, first_near, far_body, 0)
    lax.fori_loop(first_near, i + 1, near_body, 0)

    for h in range(n_heads):
        rows = slice(h * tile, (h + 1) * tile)
        l = jnp.sum(l_ref[rows, :], axis=1, keepdims=True)
        o_ref[:, h * d_lat:(h + 1) * d_lat] = (acc_ref[rows, :] / l).astype(BF16)


def _dsa(qlat, qi, wi, ki2, ckv, bias_tiles, *, k_sel, tile):
    n_heads, b, s, d_lat = qlat.shape
    assert d_lat == LANES
    w_idx = qi.shape[2]
    qblk = lambda bi, i: (bi, i, 0)
    whole = lambda bi, i: (bi, 0, 0)
    search_rows = min(DSA_SEARCH_ROWS, tile)
    return pl.pallas_call(
        functools.partial(_dsa_kernel, tile=tile, k_sel=k_sel, seq=s, search_rows=search_rows,
                          strip=min(ATTN_STRIP, tile)),
        grid=(b, s // tile),
        in_specs=[
            pl.BlockSpec((n_heads, None, tile, d_lat), lambda bi, i: (0, bi, i, 0)),
            pl.BlockSpec((None, tile, w_idx), qblk),
            pl.BlockSpec((None, tile, H_IDX), qblk),
            pl.BlockSpec((None, s, LANES), whole),
            pl.BlockSpec((None, s, d_lat), whole),
            _resident(bias_tiles.shape),
        ],
        out_specs=pl.BlockSpec((None, tile, n_heads * d_lat), qblk),
        out_shape=jax.ShapeDtypeStruct((b, s, n_heads * d_lat), BF16),
        scratch_shapes=[
            pltpu.VMEM((tile, s), I32),
            pltpu.VMEM((tile, LANES), I32),
            pltpu.VMEM((n_heads * tile, tile), F32),
            pltpu.VMEM((n_heads * tile, tile), BF16),
            pltpu.VMEM((n_heads * tile, LANES), F32),
            pltpu.VMEM((n_heads * tile, LANES), F32),
            pltpu.VMEM((n_heads * tile, LANES), F32),
            pltpu.VMEM((n_heads * tile, d_lat), F32),
        ],
        compiler_params=_params(("parallel", "arbitrary")),
        name="dsa",
    )(qlat, qi, wi, ki2, ckv, bias_tiles)


def _fox_kernel(q_ref, k_ref, v_ref, cq_ref, ck_ref, o_ref,
                s_ref, p_ref, cqr_ref, alpha_ref, m_ref, l_ref, acc_ref, *, tile, strip):
    n_pairs = q_ref.shape[0]
    i = pl.program_id(1)
    sdiff = (lax.broadcasted_iota(I32, (strip, tile), 1)
             - lax.broadcasted_iota(I32, (strip, tile), 0))
    lane = lax.broadcasted_iota(I32, (tile, LANES), 1)
    low_half = lane < HEAD_DIM
    zero = jnp.zeros((), BF16)
    q_all = []
    for pr in range(n_pairs):
        q = q_ref[pr]
        q_all.append(jnp.concatenate([jnp.where(low_half, q, zero), jnp.where(low_half, zero, q)], axis=0))
        for h in range(2):
            cqr_ref[(2 * pr + h) * tile:(2 * pr + h + 1) * tile, :] = jnp.broadcast_to(cq_ref[pr, h], (tile, LANES))
    m_ref[...] = jnp.full(m_ref.shape, NEG, F32)
    l_ref[...] = jnp.zeros(l_ref.shape, F32)
    acc_ref[...] = jnp.zeros(acc_ref.shape, F32)

    def chunk(kc, diagonal):
        cols = pl.ds(pl.multiple_of(kc * tile, tile), tile)
        for pr in range(n_pairs):
            s_ref[pr * 2 * tile:(pr + 1) * 2 * tile, :] = lax.dot_general(
                q_all[pr], k_ref[pr, cols, :], _NT, preferred_element_type=F32)
        ck = [ck_ref[pr, :, cols] for pr in range(n_pairs)]
        for r0 in range(0, tile, strip):
            for pr in range(n_pairs):
                for h in range(2):
                    rows = pl.ds((2 * pr + h) * tile + r0, strip)
                    s = s_ref[rows, :] + jnp.tile(cqr_ref[rows, :], (1, tile // LANES)) - ck[pr][h:h + 1, :]
                    if diagonal:
                        s = jnp.where(sdiff <= r0, s, NEG)
                    _softmax_strip(s, rows, p_ref, alpha_ref, m_ref, l_ref)
        for pr in range(n_pairs):
            rows = slice(pr * 2 * tile, (pr + 1) * 2 * tile)
            acc_ref[rows, :] = alpha_ref[rows, :] * acc_ref[rows, :] + jnp.dot(
                p_ref[rows, :], v_ref[pr, cols, :], preferred_element_type=F32)

    def body(kc, carry):
        chunk(kc, False)
        return carry

    lax.fori_loop(0, i, body, 0)
    chunk(i, True)
    for pr in range(n_pairs):
        r0 = 2 * pr * tile
        o0 = acc_ref[r0:r0 + tile, :] / jnp.sum(l_ref[r0:r0 + tile, :], axis=1, keepdims=True)
        o1 = acc_ref[r0 + tile:r0 + 2 * tile, :] / jnp.sum(l_ref[r0 + tile:r0 + 2 * tile, :], axis=1, keepdims=True)
        o_ref[pr] = jnp.where(low_half, o0, o1).astype(BF16)


def _fox(qf, kf, vf, cum_col, cum_row, *, tile):
    npair, b, s, _ = qf.shape
    rows = 2 * npair * tile
    qblk = lambda bi, i: (0, bi, i, 0)
    whole = lambda bi, i: (0, bi, 0, 0)
    return pl.pallas_call(
        functools.partial(_fox_kernel, tile=tile, strip=min(ATTN_STRIP, tile)),
        grid=(b, s // tile),
        in_specs=[
            pl.BlockSpec((npair, None, tile, LANES), qblk),
            pl.BlockSpec((npair, None, s, LANES), whole),
            pl.BlockSpec((npair, None, s, LANES), whole),
            pl.BlockSpec((None, npair, 2, tile, 1), lambda bi, i: (bi, 0, 0, i, 0)),
            pl.BlockSpec((None, npair, 2, s), lambda bi, i: (bi, 0, 0, 0)),
        ],
        out_specs=pl.BlockSpec((npair, None, tile, LANES), qblk),
        out_shape=jax.ShapeDtypeStruct((npair, b, s, LANES), BF16),
        scratch_shapes=[
            pltpu.VMEM((rows, tile), F32),
            pltpu.VMEM((rows, tile), BF16),
            pltpu.VMEM((rows, LANES), F32),
            pltpu.VMEM((rows, LANES), F32),
            pltpu.VMEM((rows, LANES), F32),
            pltpu.VMEM((rows, LANES), F32),
            pltpu.VMEM((rows, LANES), F32),
        ],
        compiler_params=_params(("parallel", "arbitrary")),
        name="fox",
    )(qf, kf, vf, cum_col, cum_row)


def _mix_out_kernel(h_ref, ol_ref, of_ref, bdv_ref, wo_ref, o_ref):
    oa = jnp.dot(ol_ref[...], bdv_ref[...], preferred_element_type=F32).astype(BF16)
    w_dsa = oa.shape[1]
    y = jnp.dot(oa, wo_ref[0:w_dsa, :], preferred_element_type=F32)
    for hp in range(of_ref.shape[0]):
        r0 = w_dsa + hp * LANES
        y = y + jnp.dot(of_ref[hp], wo_ref[r0:r0 + LANES, :], preferred_element_type=F32)
    o_ref[...] = h_ref[...] + y


def _mix_out(h, olat, of, bdv, wo, *, tm=512):
    t, d = h.shape
    tm = min(tm, t)
    npair = of.shape[0]
    row = lambda i: (i, 0)
    return pl.pallas_call(
        _mix_out_kernel,
        grid=(t // tm,),
        in_specs=[
            pl.BlockSpec((tm, d), row),
            pl.BlockSpec((tm, olat.shape[1]), row),
            pl.BlockSpec((npair, tm, LANES), lambda i: (0, i, 0)),
            _resident(bdv.shape),
            _resident(wo.shape),
        ],
        out_specs=pl.BlockSpec((tm, d), row),
        out_shape=jax.ShapeDtypeStruct((t, d), F32),
        compiler_params=_params(("parallel",)),
        name="mix_out",
    )(h, olat, of, bdv, wo)


def _pad_cols(w, width):
    return jnp.pad(w, ((0, 0), (0, width - w.shape[1])))


def _mixer_weights(w_in, w_uk, w_uv, n_fox):
    d_lat, n_dsa, hd = w_uk.shape
    w_dsa = n_dsa * hd
    w_idx = H_IDX * D_IDX
    w_fox = n_fox * hd
    sizes = [w_dsa, d_lat, w_idx, D_IDX, H_IDX, 3 * w_fox, n_fox]
    assert sum(sizes) == w_in.shape[1] and d_lat == LANES and 2 * D_IDX == LANES
    offs = [0]
    for sz in sizes:
        offs.append(offs[-1] + sz)
    q_a, c_raw, q_i, k_i, w_i, qkv_f, f_l = [w_in[:, offs[j]:offs[j + 1]] for j in range(7)]
    w2 = jnp.concatenate(
        [q_a, c_raw, q_i, k_i, k_i, _pad_cols(w_i, LANES), _pad_cols(f_l, LANES), qkv_f], axis=1).astype(BF16)
    eye = jnp.eye(n_dsa, dtype=w_uk.dtype)
    bdk = jnp.einsum('chd,hg->hdgc', w_uk, eye).reshape(w_dsa, n_dsa * d_lat).astype(BF16)
    bdv = jnp.einsum('chd,hg->hcgd', w_uv, eye).reshape(n_dsa * d_lat, w_dsa).astype(BF16)
    return w2, bdk, bdv, dict(w_dsa=w_dsa, w_idx=w_idx, w_fox=w_fox, d_lat=d_lat)


def kernel(x, ffn1_norm, ffn1_gate, ffn1_up, ffn1_down, mix_norm, w_in, b_forget, ckv_norm, w_uk, w_uv, w_out,
           ffn2_norm, ffn2_gate, ffn2_up, ffn2_down, rel_bias, final_norm):
    b, s, d = x.shape
    depth = w_in.shape[0]
    n_fox = b_forget.shape[1]
    t = b * s
    tile = min(DSA_TILE, s)
    fox_tile = min(FOX_TILE, s)
    k_sel = min(TOPK_MAX, s // 4)
    npair = n_fox * HEAD_DIM // LANES

    bias_tiles = _bias_tiles(rel_bias, tile)
    fg = final_norm.reshape(1, d)
    h = x.reshape(t, d)
    for l in range(depth):
        h = _ffn(h, ffn1_norm[l].reshape(1, d), ffn1_gate[l].astype(BF16), ffn1_up[l].astype(BF16),
                 ffn1_down[l].astype(BF16), fg, final=False)

        w2, bdk, bdv, dims = _mixer_weights(w_in[l], w_uk[l], w_uv[l], n_fox)
        qlat, ckv, qi, ki2, wi, logf, qf, kf, vf = _mix_in(
            h, mix_norm[l].reshape(1, d), w2, ckv_norm[l].reshape(1, -1), bdk, b_forget[l].reshape(1, n_fox), **dims)

        cum_row = _cumsum_rows(jnp.swapaxes(logf.reshape(b, s, n_fox), 1, 2).reshape(b * n_fox, s))
        cum_row = cum_row.reshape(b, npair, 2, s)
        cum_col = cum_row[..., None]

        olat = _dsa(qlat.reshape(-1, b, s, LANES), qi.reshape(b, s, -1), wi.reshape(b, s, -1),
                    ki2.reshape(b, s, -1), ckv.reshape(b, s, -1), bias_tiles, k_sel=k_sel, tile=tile)
        of = _fox(qf.reshape(npair, b, s, LANES), kf.reshape(npair, b, s, LANES), vf.reshape(npair, b, s, LANES),
                  cum_col, cum_row, tile=fox_tile)

        h = _mix_out(h, olat.reshape(t, -1), of.reshape(npair, t, LANES), bdv, w_out[l].astype(BF16))
        h = _ffn(h, ffn2_norm[l].reshape(1, d), ffn2_gate[l].astype(BF16), ffn2_up[l].astype(BF16),
                 ffn2_down[l].astype(BF16), fg, final=(l == depth - 1))
    return h.reshape(b, s, d)
```

```python
import functools
import math

import jax
import jax.numpy as jnp
from jax import lax
from jax.experimental import pallas as pl
from jax.experimental.pallas import tpu as pltpu

HEAD_DIM = 64
H_IDX = 8
D_IDX = 64
TOPK_MAX = 256
MAX_DISTANCE = 128
EPS = 1e-6
NEG = -1e30

LANES = 128
V7X_VMEM_LIMIT = 56 * 1024 * 1024

DSA_TILE = 256
FOX_TILE = 256
ATTN_STRIP = 32
COUNT_ROWS = 64
FAR_TILES = 2
N_BIAS_TILES = 3

LOG2E = math.log2(math.e)
INT_MIN = -(2 ** 31)
F32 = jnp.float32
BF16 = jnp.bfloat16
I32 = jnp.int32

_NT = (((1,), (1,)), ((), ()))


def _params(sem):
    return pltpu.CompilerParams(dimension_semantics=sem, vmem_limit_bytes=V7X_VMEM_LIMIT)


def _resident(shape):
    zeros = (0,) * len(shape)
    return pl.BlockSpec(shape, lambda *_: zeros, pipeline_mode=pl.Buffered(1))


def _rms(x, g):
    y = x * lax.rsqrt(jnp.mean(x * x, axis=-1, keepdims=True) + EPS)
    return y * g


def _ffn_kernel(x_ref, g_ref, wg_ref, wu_ref, wd_ref, fg_ref, o_ref, acc_ref, *, ff_chunk, final):
    x = x_ref[...]
    xn = _rms(x, g_ref[...]).astype(BF16)
    d_ff = wg_ref.shape[1]
    for c0 in range(0, d_ff, ff_chunk):
        c1 = min(c0 + ff_chunk, d_ff)
        g = jnp.dot(xn, wg_ref[:, c0:c1], preferred_element_type=F32)
        u = jnp.dot(xn, wu_ref[:, c0:c1], preferred_element_type=F32)
        a = (g * jax.nn.sigmoid(g) * u).astype(BF16)
        d = jnp.dot(a, wd_ref[c0:c1, :], preferred_element_type=F32)
        if c0 == 0:
            acc_ref[...] = d
        else:
            acc_ref[...] += d
    out = x + 0.5 * acc_ref[...]
    if final:
        out = _rms(out, fg_ref[...])
    o_ref[...] = out


def _ffn(h, g, wg, wu, wd, fg, *, final, tm=512, ff_chunk=256):
    t, d = h.shape
    d_ff = wg.shape[1]
    tm = min(tm, t)
    return pl.pallas_call(
        functools.partial(_ffn_kernel, ff_chunk=ff_chunk, final=final),
        grid=(t // tm,),
        in_specs=[
            pl.BlockSpec((tm, d), lambda i: (i, 0)),
            _resident((1, d)),
            _resident((d, d_ff)),
            _resident((d, d_ff)),
            _resident((d_ff, d)),
            _resident((1, d)),
        ],
        out_specs=pl.BlockSpec((tm, d), lambda i: (i, 0)),
        out_shape=jax.ShapeDtypeStruct((t, d), F32),
        scratch_shapes=[pltpu.VMEM((tm, d), F32)],
        compiler_params=_params(("parallel",)),
        name="ffn",
    )(h, g, wg, wu, wd, fg)


def _mix_in_kernel(x_ref, g_ref, w_ref, gc_ref, bdk_ref, bf_ref,
                   qlat_ref, ckv_ref, qi_ref, ki_ref, wi_ref, lf_ref, qf_ref, kf_ref, vf_ref,
                   *, w_dsa, w_idx, n_idx_heads, w_fox):
    xn = _rms(x_ref[...], g_ref[...]).astype(BF16)
    pos = [0]

    def proj(width):
        c0 = pos[0]
        pos[0] = c0 + width
        return jnp.dot(xn, w_ref[:, c0:c0 + width], preferred_element_type=F32)

    qk_scale = HEAD_DIM ** -0.5 * LOG2E
    idx_scale = D_IDX ** -0.5
    qa = proj(w_dsa).astype(BF16)
    qlat = jnp.dot(qa, bdk_ref[...], preferred_element_type=F32)
    for h in range(qlat_ref.shape[0]):
        qlat_ref[h] = (qlat[:, h * LANES:(h + 1) * LANES] * qk_scale).astype(BF16)
    ckv_ref[...] = _rms(proj(LANES), gc_ref[...]).astype(BF16)
    qi_ref[...] = (proj(w_idx) * idx_scale).astype(BF16)
    ki_ref[...] = proj(LANES).astype(BF16)
    wi_ref[...] = proj(LANES)[:, :n_idx_heads] * (n_idx_heads ** -0.5)
    n_fox = bf_ref.shape[1]
    f = proj(LANES)[:, :n_fox] + bf_ref[...]
    lf_ref[...] = jnp.minimum(f, 0.0) - jnp.log1p(jnp.exp(-jnp.abs(f)))
    for hp in range(w_fox // LANES):
        qf_ref[hp] = (proj(LANES) * qk_scale).astype(BF16)
    for hp in range(w_fox // LANES):
        kf_ref[hp] = proj(LANES).astype(BF16)
    for hp in range(w_fox // LANES):
        vf_ref[hp] = proj(LANES).astype(BF16)


def _mix_in(h, g, w2, gc, bdk, bfg, *, w_dsa, w_idx, w_fox, d_lat, tm=512):
    t, d = h.shape
    tm = min(tm, t)
    n_dsa = w_dsa // HEAD_DIM
    n_fox = bfg.shape[1]
    npair = w_fox // LANES
    row = lambda i: (i, 0)
    row3 = lambda i: (0, i, 0)
    out_shape = [
        jax.ShapeDtypeStruct((n_dsa, t, d_lat), BF16),
        jax.ShapeDtypeStruct((t, d_lat), BF16),
        jax.ShapeDtypeStruct((t, w_idx), BF16),
        jax.ShapeDtypeStruct((t, LANES), BF16),
        jax.ShapeDtypeStruct((t, H_IDX), F32),
        jax.ShapeDtypeStruct((t, n_fox), F32),
        jax.ShapeDtypeStruct((npair, t, LANES), BF16),
        jax.ShapeDtypeStruct((npair, t, LANES), BF16),
        jax.ShapeDtypeStruct((npair, t, LANES), BF16),
    ]
    out_specs = [
        pl.BlockSpec((n_dsa, tm, d_lat), row3),
        pl.BlockSpec((tm, d_lat), row),
        pl.BlockSpec((tm, w_idx), row),
        pl.BlockSpec((tm, LANES), row),
        pl.BlockSpec((tm, H_IDX), row),
        pl.BlockSpec((tm, n_fox), row),
        pl.BlockSpec((npair, tm, LANES), row3),
        pl.BlockSpec((npair, tm, LANES), row3),
        pl.BlockSpec((npair, tm, LANES), row3),
    ]
    return pl.pallas_call(
        functools.partial(_mix_in_kernel, w_dsa=w_dsa, w_idx=w_idx, n_idx_heads=H_IDX, w_fox=w_fox),
        grid=(t // tm,),
        in_specs=[
            pl.BlockSpec((tm, d), row),
            _resident((1, d)),
            _resident(w2.shape),
            _resident((1, d_lat)),
            _resident(bdk.shape),
            _resident((1, n_fox)),
        ],
        out_specs=out_specs,
        out_shape=out_shape,
        compiler_params=_params(("parallel",)),
        name="mix_in",
    )(h, g, w2, gc, bdk, bfg)


def _cumsum_kernel(x_ref, o_ref):
    x = x_ref[...]
    n = x.shape[1]
    lane = lax.broadcasted_iota(I32, x.shape, 1)
    sh = 1
    while sh < n:
        x = x + jnp.where(lane >= sh, pltpu.roll(x, sh, axis=1), 0.0)
        sh *= 2
    o_ref[...] = x * LOG2E


def _cumsum_rows(x):
    r, n = x.shape
    rb = 8 if r % 8 == 0 else r
    return pl.pallas_call(
        _cumsum_kernel,
        grid=(r // rb,),
        in_specs=[pl.BlockSpec((rb, n), lambda i: (i, 0))],
        out_specs=pl.BlockSpec((rb, n), lambda i: (i, 0)),
        out_shape=jax.ShapeDtypeStruct((r, n), F32),
        compiler_params=_params(("parallel",)),
        name="cumsum",
    )(x)


def _bias_tiles_kernel(rb_ref, bucket_ref, o_ref):
    n_buckets, n_heads = rb_ref.shape
    for j in range(bucket_ref.shape[0]):
        bucket = bucket_ref[j]
        for h in range(n_heads):
            acc = jnp.zeros(bucket.shape, F32)
            for b in range(n_buckets):
                acc = jnp.where(bucket == b, rb_ref[b, h], acc)
            o_ref[j, h] = (acc - rb_ref[n_buckets - 1, h]) * LOG2E


def _t5_bucket(dist, n_buckets):
    n = jnp.maximum(dist, 0)
    max_exact = n_buckets // 2
    nf = jnp.maximum(n, 1).astype(F32)
    large = max_exact + (jnp.log(nf / max_exact) / math.log(MAX_DISTANCE / max_exact)
                         * (n_buckets - max_exact)).astype(I32)
    large = jnp.minimum(large, n_buckets - 1)
    return jnp.where(n < max_exact, n, large)


def _bias_tiles(rel_bias, tile):
    n_buckets, n_heads = rel_bias.shape
    assert tile >= MAX_DISTANCE
    assert N_BIAS_TILES == FAR_TILES + 1
    r = lax.broadcasted_iota(I32, (tile, tile), 0)
    c = lax.broadcasted_iota(I32, (tile, tile), 1)
    buckets = jnp.stack([_t5_bucket(j * tile + r - c, n_buckets) for j in range(N_BIAS_TILES)])
    return pl.pallas_call(
        _bias_tiles_kernel,
        in_specs=[pl.BlockSpec(memory_space=pltpu.SMEM), pl.BlockSpec(memory_space=pltpu.VMEM)],
        out_specs=pl.BlockSpec(memory_space=pltpu.VMEM),
        out_shape=jax.ShapeDtypeStruct((N_BIAS_TILES, n_heads, tile, tile), F32),
        compiler_params=pltpu.CompilerParams(vmem_limit_bytes=V7X_VMEM_LIMIT),
        name="bias_tiles",
    )(rel_bias, buckets)


def _softmax_strip(s, rows, p_ref, alpha_ref, m_ref, l_ref):
    width = s.shape[1]
    reps = width // LANES
    m_old = m_ref[rows, :]
    m_new = jnp.maximum(m_old, jnp.max(s, axis=1, keepdims=True))
    alpha = jnp.exp2(m_old - m_new)
    p = jnp.exp2(s - jnp.tile(m_new, (1, reps)))
    psum = p[:, 0:LANES]
    for j in range(1, reps):
        psum = psum + p[:, j * LANES:(j + 1) * LANES]
    l_ref[rows, :] = alpha * l_ref[rows, :] + psum
    alpha_ref[rows, :] = alpha
    m_ref[rows, :] = m_new
    p_ref[rows, 0:width] = p.astype(BF16)


def _dsa_kernel(qlat_ref, qi_ref, wi_ref, ki_ref, ckv_ref, bias_ref, o_ref,
                keys_ref, thr_ref, cand_ref, cnt_ref, s_ref, p_ref, alpha_ref, m_ref, l_ref, acc_ref,
                *, tile, k_sel, seq, strip):
    n_heads, _, d_lat = qlat_ref.shape
    i = pl.program_id(1)
    n_chunks = i + 1
    row = lax.broadcasted_iota(I32, (tile, tile), 0)
    col = lax.broadcasted_iota(I32, (tile, tile), 1)
    diff = col - row
    lane = lax.broadcasted_iota(I32, (tile, LANES), 1)
    low_half = lane < D_IDX

    def chunk_start(kc):
        return pl.multiple_of(kc * tile, tile)

    qi = qi_ref[...]
    wi = wi_ref[...]
    zero = jnp.zeros((), BF16)
    qm = []
    for h in range(H_IDX):
        grp = qi[:, (h // 2) * LANES:(h // 2 + 1) * LANES]
        qm.append(jnp.where(low_half, grp, zero) if h % 2 == 0 else jnp.where(low_half, zero, grp))

    def score_chunk(kc, causal):
        off = chunk_start(kc)
        k2 = ki_ref[pl.ds(off, tile), :]
        acc = None
        for h in range(H_IDX):
            r = lax.dot_general(qm[h], k2, _NT, preferred_element_type=F32)
            term = jnp.maximum(r, 0.0) * wi[:, h:h + 1]
            acc = term if acc is None else acc + term
        if causal:
            acc = jnp.where(diff <= 0, acc, NEG)
        bits = pltpu.bitcast(acc, I32)
        keys_ref[:, pl.ds(off, tile)] = jnp.where(bits < 0, INT_MIN - bits, bits)

    def score_body(kc, carry):
        score_chunk(kc, False)
        return carry

    lax.fori_loop(0, i, score_body, 0)
    score_chunk(i, True)

    reps = tile // LANES

    def wide(ref):
        return jnp.tile(ref[...], (1, reps))

    def count_ge_cand():
        cnt_ref[...] = jnp.zeros(cnt_ref.shape, I32)

        def body(c, carry):
            cols = pl.ds(chunk_start(c), tile)
            for r0 in range(0, tile, COUNT_ROWS):
                rows = pl.ds(r0, COUNT_ROWS)
                m = (keys_ref[rows, cols] >= jnp.tile(cand_ref[rows, :], (1, reps))).astype(I32)
                part = m[:, 0:LANES]
                for j in range(1, reps):
                    part = part + m[:, j * LANES:(j + 1) * LANES]
                cnt_ref[rows, :] += part
            return carry

        lax.fori_loop(0, n_chunks, body, 0)
        return jnp.broadcast_to(jnp.sum(cnt_ref[...], axis=1, keepdims=True), (tile, LANES))

    def count(pred):
        def body(c, cnt):
            m = pred(keys_ref[:, pl.ds(chunk_start(c), tile)], c).astype(I32)
            part = m[:, 0:LANES]
            for j in range(1, reps):
                part = part + m[:, j * LANES:(j + 1) * LANES]
            return cnt + part
        cnt = lax.fori_loop(0, n_chunks, body, jnp.zeros((tile, LANES), I32))
        return jnp.broadcast_to(jnp.sum(cnt, axis=1, keepdims=True), (tile, LANES))

    cand_ref[...] = jnp.zeros(cand_ref.shape, I32)
    thr_ref[...] = jnp.where(count_ge_cand() >= k_sel, 0, INT_MIN).astype(I32)

    def bit_body(it, carry):
        cand_ref[...] = thr_ref[...] | lax.shift_left(jnp.int32(1), 30 - it)
        cnt = count_ge_cand()
        thr_ref[...] = jnp.where(cnt >= k_sel, cand_ref[...], thr_ref[...])
        return carry

    lax.fori_loop(0, 31, bit_body, 0)

    cand_ref[...] = thr_ref[...]
    n_ge = count_ge_cand()

    @pl.when(jnp.max(n_ge) > k_sel)
    def _():
        need = k_sel - count(lambda k, c: k > wide(thr_ref))
        pos_bits = max(seq - 1, 1).bit_length()

        def pos_body(it, x):
            cand = x | lax.shift_left(jnp.int32(1), pos_bits - 1 - it)
            cand_ref[...] = cand
            f = count(lambda k, c: (k == wide(thr_ref)) & (col + c * tile < wide(cand_ref)))
            return jnp.where(f < need, cand, x)

        cand_ref[...] = lax.fori_loop(0, pos_bits, pos_body, jnp.zeros((tile, LANES), I32))

        def drop_body(c, carry):
            cols = pl.ds(chunk_start(c), tile)
            k = keys_ref[:, cols]
            drop = (k == wide(thr_ref)) & (col + c * tile > wide(cand_ref))
            keys_ref[:, cols] = jnp.where(drop, INT_MIN, k)
            return carry

        lax.fori_loop(0, n_chunks, drop_body, 0)

    m_ref[...] = jnp.full(m_ref.shape, NEG, F32)
    l_ref[...] = jnp.zeros(l_ref.shape, F32)
    acc_ref[...] = jnp.zeros(acc_ref.shape, F32)
    q_all = qlat_ref[...].reshape(n_heads * tile, d_lat)
    sdiff = diff[0:strip, :]

    def attn_chunk(start, width, j):
        cols = pl.ds(start, width)
        c = ckv_ref[cols, :]
        s_ref[:, 0:width] = lax.dot_general(q_all, c, _NT, preferred_element_type=F32)
        for r0 in range(0, tile, strip):
            qrows = pl.ds(r0, strip)
            sel = keys_ref[qrows, cols] >= jnp.tile(thr_ref[qrows, :], (1, width // LANES))
            if j is not None:
                sel = sel & (sdiff - r0 <= j * tile)
            for h in range(n_heads):
                rows = pl.ds(h * tile + r0, strip)
                s = s_ref[rows, 0:width]
                if j is not None:
                    s = s + bias_ref[j, h, qrows, :]
                _softmax_strip(jnp.where(sel, s, NEG), rows, p_ref, alpha_ref, m_ref, l_ref)
        acc_ref[...] = alpha_ref[...] * acc_ref[...] + jnp.dot(p_ref[:, 0:width], c,
                                                                preferred_element_type=F32)

    far_width = FAR_TILES * tile

    def far_body(g, carry):
        attn_chunk(pl.multiple_of(g * far_width, far_width), far_width, None)
        return carry

    def near_body(kc, carry):
        attn_chunk(chunk_start(kc), tile, i - kc)
        return carry

    n_far_groups = jnp.maximum(i - 1, 0) // FAR_TILES
    lax.fori_loop(0, n_far_groups, far_body, 0)
    lax.fori_loop(n_far_groups * FAR_TILES, i + 1, near_body, 0)

    for h in range(n_heads):
        rows = slice(h * tile, (h + 1) * tile)
        l = jnp.sum(l_ref[rows, :], axis=1, keepdims=True)
        o_ref[:, h * d_lat:(h + 1) * d_lat] = (acc_ref[rows, :] / l).astype(BF16)


def _dsa(qlat, qi, wi, ki2, ckv, bias_tiles, *, k_sel, tile):
    n_heads, b, s, d_lat = qlat.shape
    assert d_lat == LANES
    w_idx = qi.shape[2]
    qblk = lambda bi, i: (bi, i, 0)
    whole = lambda bi, i: (bi, 0, 0)
    return pl.pallas_call(
        functools.partial(_dsa_kernel, tile=tile, k_sel=k_sel, seq=s, strip=min(ATTN_STRIP, tile)),
        grid=(b, s // tile),
        in_specs=[
            pl.BlockSpec((n_heads, None, tile, d_lat), lambda bi, i: (0, bi, i, 0)),
            pl.BlockSpec((None, tile, w_idx), qblk),
            pl.BlockSpec((None, tile, H_IDX), qblk),
            pl.BlockSpec((None, s, LANES), whole),
            pl.BlockSpec((None, s, d_lat), whole),
            _resident(bias_tiles.shape),
        ],
        out_specs=pl.BlockSpec((None, tile, n_heads * d_lat), qblk),
        out_shape=jax.ShapeDtypeStruct((b, s, n_heads * d_lat), BF16),
        scratch_shapes=[
            pltpu.VMEM((tile, s), I32),
            pltpu.VMEM((tile, LANES), I32),
            pltpu.VMEM((tile, LANES), I32),
            pltpu.VMEM((tile, LANES), I32),
            pltpu.VMEM((n_heads * tile, FAR_TILES * tile), F32),
            pltpu.VMEM((n_heads * tile, FAR_TILES * tile), BF16),
            pltpu.VMEM((n_heads * tile, LANES), F32),
            pltpu.VMEM((n_heads * tile, LANES), F32),
            pltpu.VMEM((n_heads * tile, LANES), F32),
            pltpu.VMEM((n_heads * tile, d_lat), F32),
        ],
        compiler_params=_params(("parallel", "arbitrary")),
        name="dsa",
    )(qlat, qi, wi, ki2, ckv, bias_tiles)


def _fox_kernel(q_ref, k_ref, v_ref, cq_ref, ck_ref, o_ref,
                s_ref, p_ref, cqr_ref, alpha_ref, m_ref, l_ref, acc_ref, *, tile, strip):
    n_pairs = q_ref.shape[0]
    i = pl.program_id(1)
    sdiff = (lax.broadcasted_iota(I32, (strip, tile), 1)
             - lax.broadcasted_iota(I32, (strip, tile), 0))
    lane = lax.broadcasted_iota(I32, (tile, LANES), 1)
    low_half = lane < HEAD_DIM
    zero = jnp.zeros((), BF16)
    q_all = []
    for pr in range(n_pairs):
        q = q_ref[pr]
        q_all.append(jnp.concatenate([jnp.where(low_half, q, zero), jnp.where(low_half, zero, q)], axis=0))
        for h in range(2):
            cqr_ref[(2 * pr + h) * tile:(2 * pr + h + 1) * tile, :] = jnp.broadcast_to(cq_ref[pr, h], (tile, LANES))
    m_ref[...] = jnp.full(m_ref.shape, NEG, F32)
    l_ref[...] = jnp.zeros(l_ref.shape, F32)
    acc_ref[...] = jnp.zeros(acc_ref.shape, F32)

    def chunk(start, width, j):
        cols = pl.ds(start, width)
        for pr in range(n_pairs):
            s_ref[pr * 2 * tile:(pr + 1) * 2 * tile, 0:width] = lax.dot_general(
                q_all[pr], k_ref[pr, cols, :], _NT, preferred_element_type=F32)
        ck = [ck_ref[pr, :, cols] for pr in range(n_pairs)]
        for r0 in range(0, tile, strip):
            for pr in range(n_pairs):
                for h in range(2):
                    rows = pl.ds((2 * pr + h) * tile + r0, strip)
                    s = s_ref[rows, 0:width] + jnp.tile(cqr_ref[rows, :], (1, width // LANES)) - ck[pr][h:h + 1, :]
                    if j is not None:
                        s = jnp.where(sdiff - r0 <= j * tile, s, NEG)
                    _softmax_strip(s, rows, p_ref, alpha_ref, m_ref, l_ref)
        for pr in range(n_pairs):
            rows = slice(pr * 2 * tile, (pr + 1) * 2 * tile)
            acc_ref[rows, :] = alpha_ref[rows, :] * acc_ref[rows, :] + jnp.dot(
                p_ref[rows, 0:width], v_ref[pr, cols, :], preferred_element_type=F32)

    far_width = FAR_TILES * tile

    def far_body(g, carry):
        chunk(pl.multiple_of(g * far_width, far_width), far_width, None)
        return carry

    def near_body(kc, carry):
        chunk(pl.multiple_of(kc * tile, tile), tile, i - kc)
        return carry

    n_far_groups = i // FAR_TILES
    lax.fori_loop(0, n_far_groups, far_body, 0)
    lax.fori_loop(n_far_groups * FAR_TILES, i + 1, near_body, 0)
    for pr in range(n_pairs):
        r0 = 2 * pr * tile
        o0 = acc_ref[r0:r0 + tile, :] / jnp.sum(l_ref[r0:r0 + tile, :], axis=1, keepdims=True)
        o1 = acc_ref[r0 + tile:r0 + 2 * tile, :] / jnp.sum(l_ref[r0 + tile:r0 + 2 * tile, :], axis=1, keepdims=True)
        o_ref[pr] = jnp.where(low_half, o0, o1).astype(BF16)


def _fox(qf, kf, vf, cum_col, cum_row, *, tile):
    npair, b, s, _ = qf.shape
    rows = 2 * npair * tile
    qblk = lambda bi, i: (0, bi, i, 0)
    whole = lambda bi, i: (0, bi, 0, 0)
    return pl.pallas_call(
        functools.partial(_fox_kernel, tile=tile, strip=min(ATTN_STRIP, tile)),
        grid=(b, s // tile),
        in_specs=[
            pl.BlockSpec((npair, None, tile, LANES), qblk),
            pl.BlockSpec((npair, None, s, LANES), whole),
            pl.BlockSpec((npair, None, s, LANES), whole),
            pl.BlockSpec((None, npair, 2, tile, 1), lambda bi, i: (bi, 0, 0, i, 0)),
            pl.BlockSpec((None, npair, 2, s), lambda bi, i: (bi, 0, 0, 0)),
        ],
        out_specs=pl.BlockSpec((npair, None, tile, LANES), qblk),
        out_shape=jax.ShapeDtypeStruct((npair, b, s, LANES), BF16),
        scratch_shapes=[
            pltpu.VMEM((rows, FAR_TILES * tile), F32),
            pltpu.VMEM((rows, FAR_TILES * tile), BF16),
            pltpu.VMEM((rows, LANES), F32),
            pltpu.VMEM((rows, LANES), F32),
            pltpu.VMEM((rows, LANES), F32),
            pltpu.VMEM((rows, LANES), F32),
            pltpu.VMEM((rows, LANES), F32),
        ],
        compiler_params=_params(("parallel", "arbitrary")),
        name="fox",
    )(qf, kf, vf, cum_col, cum_row)


def _mix_out_kernel(h_ref, ol_ref, of_ref, bdv_ref, wo_ref, o_ref):
    oa = jnp.dot(ol_ref[...], bdv_ref[...], preferred_element_type=F32).astype(BF16)
    w_dsa = oa.shape[1]
    y = jnp.dot(oa, wo_ref[0:w_dsa, :], preferred_element_type=F32)
    for hp in range(of_ref.shape[0]):
        r0 = w_dsa + hp * LANES
        y = y + jnp.dot(of_ref[hp], wo_ref[r0:r0 + LANES, :], preferred_element_type=F32)
    o_ref[...] = h_ref[...] + y


def _mix_out(h, olat, of, bdv, wo, *, tm=512):
    t, d = h.shape
    tm = min(tm, t)
    npair = of.shape[0]
    row = lambda i: (i, 0)
    return pl.pallas_call(
        _mix_out_kernel,
        grid=(t // tm,),
        in_specs=[
            pl.BlockSpec((tm, d), row),
            pl.BlockSpec((tm, olat.shape[1]), row),
            pl.BlockSpec((npair, tm, LANES), lambda i: (0, i, 0)),
            _resident(bdv.shape),
            _resident(wo.shape),
        ],
        out_specs=pl.BlockSpec((tm, d), row),
        out_shape=jax.ShapeDtypeStruct((t, d), F32),
        compiler_params=_params(("parallel",)),
        name="mix_out",
    )(h, olat, of, bdv, wo)


def _pad_cols(w, width):
    return jnp.pad(w, ((0, 0), (0, width - w.shape[1])))


def _mixer_weights(w_in, w_uk, w_uv, n_fox):
    d_lat, n_dsa, hd = w_uk.shape
    w_dsa = n_dsa * hd
    w_idx = H_IDX * D_IDX
    w_fox = n_fox * hd
    sizes = [w_dsa, d_lat, w_idx, D_IDX, H_IDX, 3 * w_fox, n_fox]
    assert sum(sizes) == w_in.shape[1] and d_lat == LANES and 2 * D_IDX == LANES
    offs = [0]
    for sz in sizes:
        offs.append(offs[-1] + sz)
    q_a, c_raw, q_i, k_i, w_i, qkv_f, f_l = [w_in[:, offs[j]:offs[j + 1]] for j in range(7)]
    w2 = jnp.concatenate(
        [q_a, c_raw, q_i, k_i, k_i, _pad_cols(w_i, LANES), _pad_cols(f_l, LANES), qkv_f], axis=1).astype(BF16)
    eye = jnp.eye(n_dsa, dtype=w_uk.dtype)
    bdk = jnp.einsum('chd,hg->hdgc', w_uk, eye).reshape(w_dsa, n_dsa * d_lat).astype(BF16)
    bdv = jnp.einsum('chd,hg->hcgd', w_uv, eye).reshape(n_dsa * d_lat, w_dsa).astype(BF16)
    return w2, bdk, bdv, dict(w_dsa=w_dsa, w_idx=w_idx, w_fox=w_fox, d_lat=d_lat)


def kernel(x, ffn1_norm, ffn1_gate, ffn1_up, ffn1_down, mix_norm, w_in, b_forget, ckv_norm, w_uk, w_uv, w_out,
           ffn2_norm, ffn2_gate, ffn2_up, ffn2_down, rel_bias, final_norm):
    b, s, d = x.shape
    depth = w_in.shape[0]
    n_fox = b_forget.shape[1]
    t = b * s
    tile = min(DSA_TILE, s)
    fox_tile = min(FOX_TILE, s)
    k_sel = min(TOPK_MAX, s // 4)
    npair = n_fox * HEAD_DIM // LANES

    bias_tiles = _bias_tiles(rel_bias, tile)
    fg = final_norm.reshape(1, d)
    h = x.reshape(t, d)
    for l in range(depth):
        h = _ffn(h, ffn1_norm[l].reshape(1, d), ffn1_gate[l].astype(BF16), ffn1_up[l].astype(BF16),
                 ffn1_down[l].astype(BF16), fg, final=False)

        w2, bdk, bdv, dims = _mixer_weights(w_in[l], w_uk[l], w_uv[l], n_fox)
        qlat, ckv, qi, ki2, wi, logf, qf, kf, vf = _mix_in(
            h, mix_norm[l].reshape(1, d), w2, ckv_norm[l].reshape(1, -1), bdk, b_forget[l].reshape(1, n_fox), **dims)

        cum_row = _cumsum_rows(jnp.swapaxes(logf.reshape(b, s, n_fox), 1, 2).reshape(b * n_fox, s))
        cum_row = cum_row.reshape(b, npair, 2, s)
        cum_col = cum_row[..., None]

        olat = _dsa(qlat.reshape(-1, b, s, LANES), qi.reshape(b, s, -1), wi.reshape(b, s, -1),
                    ki2.reshape(b, s, -1), ckv.reshape(b, s, -1), bias_tiles, k_sel=k_sel, tile=tile)
        of = _fox(qf.reshape(npair, b, s, LANES), kf.reshape(npair, b, s, LANES), vf.reshape(npair, b, s, LANES),
                  cum_col, cum_row, tile=fox_tile)

        h = _mix_out(h, olat.reshape(t, -1), of.reshape(npair, t, LANES), bdv, w_out[l].astype(BF16))
        h = _ffn(h, ffn2_norm[l].reshape(1, d), ffn2_gate[l].astype(BF16), ffn2_up[l].astype(BF16),
                 ffn2_down[l].astype(BF16), fg, final=(l == depth - 1))
    return h.reshape(b, s, d)
```

```python
import functools
import math

import jax
import jax.numpy as jnp
from jax import lax
from jax.experimental import pallas as pl
from jax.experimental.pallas import tpu as pltpu

HEAD_DIM = 64
H_IDX = 8
D_IDX = 64
TOPK_MAX = 256
MAX_DISTANCE = 128
EPS = 1e-6
NEG = -1e30

LANES = 128
SUBLANES = 8
V7X_VMEM_LIMIT = 56 * 1024 * 1024

DSA_TILE = 256
FOX_TILE = 256
ATTN_STRIP = 32
FAR_TILES = 2
N_BIAS_TILES = 3

LOG2E = math.log2(math.e)
INT_MIN = -(2 ** 31)
F32 = jnp.float32
BF16 = jnp.bfloat16
I32 = jnp.int32

_NT = (((1,), (1,)), ((), ()))


def _params(sem):
    return pltpu.CompilerParams(dimension_semantics=sem, vmem_limit_bytes=V7X_VMEM_LIMIT)


def _resident(shape):
    zeros = (0,) * len(shape)
    return pl.BlockSpec(shape, lambda *_: zeros, pipeline_mode=pl.Buffered(1))


def _rms(x, g):
    y = x * lax.rsqrt(jnp.mean(x * x, axis=-1, keepdims=True) + EPS)
    return y * g


def _ffn_kernel(x_ref, g_ref, wg_ref, wu_ref, wd_ref, fg_ref, o_ref, acc_ref, *, ff_chunk, final):
    x = x_ref[...]
    xn = _rms(x, g_ref[...]).astype(BF16)
    d_ff = wg_ref.shape[1]
    for c0 in range(0, d_ff, ff_chunk):
        c1 = min(c0 + ff_chunk, d_ff)
        g = jnp.dot(xn, wg_ref[:, c0:c1], preferred_element_type=F32)
        u = jnp.dot(xn, wu_ref[:, c0:c1], preferred_element_type=F32)
        a = (g * jax.nn.sigmoid(g) * u).astype(BF16)
        d = jnp.dot(a, wd_ref[c0:c1, :], preferred_element_type=F32)
        if c0 == 0:
            acc_ref[...] = d
        else:
            acc_ref[...] += d
    out = x + 0.5 * acc_ref[...]
    if final:
        out = _rms(out, fg_ref[...])
    o_ref[...] = out


def _ffn(h, g, wg, wu, wd, fg, *, final, tm=512, ff_chunk=256):
    t, d = h.shape
    d_ff = wg.shape[1]
    tm = min(tm, t)
    return pl.pallas_call(
        functools.partial(_ffn_kernel, ff_chunk=ff_chunk, final=final),
        grid=(t // tm,),
        in_specs=[
            pl.BlockSpec((tm, d), lambda i: (i, 0)),
            _resident((1, d)),
            _resident((d, d_ff)),
            _resident((d, d_ff)),
            _resident((d_ff, d)),
            _resident((1, d)),
        ],
        out_specs=pl.BlockSpec((tm, d), lambda i: (i, 0)),
        out_shape=jax.ShapeDtypeStruct((t, d), F32),
        scratch_shapes=[pltpu.VMEM((tm, d), F32)],
        compiler_params=_params(("parallel",)),
        name="ffn",
    )(h, g, wg, wu, wd, fg)


def _mix_in_kernel(x_ref, g_ref, w_ref, gc_ref, bdk_ref, bf_ref,
                   qlat_ref, ckv_ref, qi_ref, ki_ref, wi_ref, lf_ref, qf_ref, kf_ref, vf_ref,
                   *, w_dsa, w_idx, n_idx_heads, w_fox):
    xn = _rms(x_ref[...], g_ref[...]).astype(BF16)
    pos = [0]

    def proj(width):
        c0 = pos[0]
        pos[0] = c0 + width
        return jnp.dot(xn, w_ref[:, c0:c0 + width], preferred_element_type=F32)

    qk_scale = HEAD_DIM ** -0.5 * LOG2E
    idx_scale = D_IDX ** -0.5
    qa = proj(w_dsa).astype(BF16)
    qlat = jnp.dot(qa, bdk_ref[...], preferred_element_type=F32)
    for h in range(qlat_ref.shape[0]):
        qlat_ref[h] = (qlat[:, h * LANES:(h + 1) * LANES] * qk_scale).astype(BF16)
    ckv_ref[...] = _rms(proj(LANES), gc_ref[...]).astype(BF16)
    qi_ref[...] = (proj(w_idx) * idx_scale).astype(BF16)
    ki_ref[...] = proj(LANES).astype(BF16)
    wi_ref[...] = proj(LANES)[:, :n_idx_heads] * (n_idx_heads ** -0.5)
    n_fox = bf_ref.shape[1]
    f = proj(LANES)[:, :n_fox] + bf_ref[...]
    lf_ref[...] = jnp.minimum(f, 0.0) - jnp.log1p(jnp.exp(-jnp.abs(f)))
    for hp in range(w_fox // LANES):
        qf_ref[hp] = (proj(LANES) * qk_scale).astype(BF16)
    for hp in range(w_fox // LANES):
        kf_ref[hp] = proj(LANES).astype(BF16)
    for hp in range(w_fox // LANES):
        vf_ref[hp] = proj(LANES).astype(BF16)


def _mix_in(h, g, w2, gc, bdk, bfg, *, w_dsa, w_idx, w_fox, d_lat, tm=512):
    t, d = h.shape
    tm = min(tm, t)
    n_dsa = w_dsa // HEAD_DIM
    n_fox = bfg.shape[1]
    npair = w_fox // LANES
    row = lambda i: (i, 0)
    row3 = lambda i: (0, i, 0)
    out_shape = [
        jax.ShapeDtypeStruct((n_dsa, t, d_lat), BF16),
        jax.ShapeDtypeStruct((t, d_lat), BF16),
        jax.ShapeDtypeStruct((t, w_idx), BF16),
        jax.ShapeDtypeStruct((t, LANES), BF16),
        jax.ShapeDtypeStruct((t, H_IDX), F32),
        jax.ShapeDtypeStruct((t, n_fox), F32),
        jax.ShapeDtypeStruct((npair, t, LANES), BF16),
        jax.ShapeDtypeStruct((npair, t, LANES), BF16),
        jax.ShapeDtypeStruct((npair, t, LANES), BF16),
    ]
    out_specs = [
        pl.BlockSpec((n_dsa, tm, d_lat), row3),
        pl.BlockSpec((tm, d_lat), row),
        pl.BlockSpec((tm, w_idx), row),
        pl.BlockSpec((tm, LANES), row),
        pl.BlockSpec((tm, H_IDX), row),
        pl.BlockSpec((tm, n_fox), row),
        pl.BlockSpec((npair, tm, LANES), row3),
        pl.BlockSpec((npair, tm, LANES), row3),
        pl.BlockSpec((npair, tm, LANES), row3),
    ]
    return pl.pallas_call(
        functools.partial(_mix_in_kernel, w_dsa=w_dsa, w_idx=w_idx, n_idx_heads=H_IDX, w_fox=w_fox),
        grid=(t // tm,),
        in_specs=[
            pl.BlockSpec((tm, d), row),
            _resident((1, d)),
            _resident(w2.shape),
            _resident((1, d_lat)),
            _resident(bdk.shape),
            _resident((1, n_fox)),
        ],
        out_specs=out_specs,
        out_shape=out_shape,
        compiler_params=_params(("parallel",)),
        name="mix_in",
    )(h, g, w2, gc, bdk, bfg)


def _cumsum_kernel(x_ref, o_ref):
    x = x_ref[...]
    n = x.shape[1]
    lane = lax.broadcasted_iota(I32, x.shape, 1)
    sh = 1
    while sh < n:
        x = x + jnp.where(lane >= sh, pltpu.roll(x, sh, axis=1), 0.0)
        sh *= 2
    o_ref[...] = x * LOG2E


def _cumsum_rows(x):
    r, n = x.shape
    rb = 8 if r % 8 == 0 else r
    return pl.pallas_call(
        _cumsum_kernel,
        grid=(r // rb,),
        in_specs=[pl.BlockSpec((rb, n), lambda i: (i, 0))],
        out_specs=pl.BlockSpec((rb, n), lambda i: (i, 0)),
        out_shape=jax.ShapeDtypeStruct((r, n), F32),
        compiler_params=_params(("parallel",)),
        name="cumsum",
    )(x)


def _bias_tiles_kernel(rb_ref, bucket_ref, o_ref):
    n_buckets, n_heads = rb_ref.shape
    for j in range(bucket_ref.shape[0]):
        bucket = bucket_ref[j]
        for h in range(n_heads):
            acc = jnp.zeros(bucket.shape, F32)
            for b in range(n_buckets):
                acc = jnp.where(bucket == b, rb_ref[b, h], acc)
            o_ref[j, h] = (acc - rb_ref[n_buckets - 1, h]) * LOG2E


def _t5_bucket(dist, n_buckets):
    n = jnp.maximum(dist, 0)
    max_exact = n_buckets // 2
    nf = jnp.maximum(n, 1).astype(F32)
    large = max_exact + (jnp.log(nf / max_exact) / math.log(MAX_DISTANCE / max_exact)
                         * (n_buckets - max_exact)).astype(I32)
    large = jnp.minimum(large, n_buckets - 1)
    return jnp.where(n < max_exact, n, large)


def _bias_tiles(rel_bias, tile):
    n_buckets, n_heads = rel_bias.shape
    assert tile >= MAX_DISTANCE
    assert N_BIAS_TILES == FAR_TILES + 1
    r = lax.broadcasted_iota(I32, (tile, tile), 0)
    c = lax.broadcasted_iota(I32, (tile, tile), 1)
    buckets = jnp.stack([_t5_bucket(j * tile + r - c, n_buckets) for j in range(N_BIAS_TILES)])
    return pl.pallas_call(
        _bias_tiles_kernel,
        in_specs=[pl.BlockSpec(memory_space=pltpu.SMEM), pl.BlockSpec(memory_space=pltpu.VMEM)],
        out_specs=pl.BlockSpec(memory_space=pltpu.VMEM),
        out_shape=jax.ShapeDtypeStruct((N_BIAS_TILES, n_heads, tile, tile), F32),
        compiler_params=pltpu.CompilerParams(vmem_limit_bytes=V7X_VMEM_LIMIT),
        name="bias_tiles",
    )(rel_bias, buckets)


def _softmax_strip(s, rows, p_ref, alpha_ref, m_ref, l_ref):
    width = s.shape[1]
    reps = width // LANES
    m_old = m_ref[rows, :]
    m_new = jnp.maximum(m_old, jnp.max(s, axis=1, keepdims=True))
    alpha = jnp.exp2(m_old - m_new)
    p = jnp.exp2(s - jnp.tile(m_new, (1, reps)))
    psum = p[:, 0:LANES]
    for j in range(1, reps):
        psum = psum + p[:, j * LANES:(j + 1) * LANES]
    l_ref[rows, :] = alpha * l_ref[rows, :] + psum
    alpha_ref[rows, :] = alpha
    m_ref[rows, :] = m_new
    p_ref[rows, 0:width] = p.astype(BF16)


def _dsa_kernel(qlat_ref, qi_ref, wi_ref, ki_ref, ckv_ref, bias_ref, o_ref,
                keys_ref, keyst_ref, thr_ref, last_ref, qidx_ref, widx_ref,
                s_ref, p_ref, alpha_ref, m_ref, l_ref, acc_ref,
                *, tile, k_sel, seq, strip):
    n_heads, _, d_lat = qlat_ref.shape
    i = pl.program_id(1)
    n_chunks = i + 1
    row = lax.broadcasted_iota(I32, (tile, tile), 0)
    col = lax.broadcasted_iota(I32, (tile, tile), 1)
    diff = col - row
    lane = lax.broadcasted_iota(I32, (tile, LANES), 1)
    low_half = lane < D_IDX

    def chunk_start(kc):
        return pl.multiple_of(kc * tile, tile)

    qi = qi_ref[...]
    wi = wi_ref[...]
    zero = jnp.zeros((), BF16)
    for h in range(H_IDX):
        grp = qi[:, (h // 2) * LANES:(h // 2 + 1) * LANES]
        qidx_ref[h * tile:(h + 1) * tile, :] = (jnp.where(low_half, grp, zero) if h % 2 == 0
                                                 else jnp.where(low_half, zero, grp))
        widx_ref[h] = jnp.broadcast_to(wi[:, h:h + 1], (tile, LANES))

    def score_chunk(start, width, j):
        k2 = ki_ref[pl.ds(start, width), :]
        acc = None
        for h in range(H_IDX):
            r = lax.dot_general(qidx_ref[h * tile:(h + 1) * tile, :], k2, _NT, preferred_element_type=F32)
            term = jnp.maximum(r, 0.0) * jnp.tile(widx_ref[h], (1, width // LANES))
            acc = term if acc is None else acc + term
        if j is not None:
            acc = jnp.where(diff <= j * tile, acc, NEG)
        bits = pltpu.bitcast(acc, I32)
        key = jnp.where(bits < 0, INT_MIN - bits, bits)
        keys_ref[:, pl.ds(start, width)] = key
        keyst_ref[pl.ds(start, width), :] = key.T

    score_width = FAR_TILES * tile

    def score_far(g, carry):
        score_chunk(pl.multiple_of(g * score_width, score_width), score_width, None)
        return carry

    def score_near(kc, carry):
        score_chunk(chunk_start(kc), tile, i - kc)
        return carry

    n_score_groups = i // FAR_TILES
    lax.fori_loop(0, n_score_groups, score_far, 0)
    lax.fori_loop(n_score_groups * FAR_TILES, i + 1, score_near, 0)

    groups = tile // SUBLANES

    def key_tile(c):
        return keyst_ref[pl.ds(chunk_start(c), tile), :].reshape(groups, SUBLANES, tile)

    def count(pred):
        def one(c, cnt):
            return cnt + jnp.sum(pred(key_tile(c), c).astype(I32), axis=0)

        def two(g, cnt):
            return one(2 * g + 1, one(2 * g, cnt))

        cnt = lax.fori_loop(0, n_chunks // 2, two, jnp.zeros((SUBLANES, tile), I32))
        cnt = lax.fori_loop(2 * (n_chunks // 2), n_chunks, one, cnt)
        return jnp.broadcast_to(jnp.sum(cnt, axis=0, keepdims=True), (SUBLANES, tile))

    def to_rows(x):
        return jnp.broadcast_to(x[0:1, :], (LANES, tile)).T

    prefix = jnp.where(count(lambda k, c: k >= 0) >= k_sel, 0, INT_MIN).astype(I32)

    def bit_body(it, prefix):
        cand = prefix | lax.shift_left(jnp.int32(1), 30 - it)
        cnt = count(lambda k, c: k >= cand[None])
        return jnp.where(cnt >= k_sel, cand, prefix)

    thr = lax.fori_loop(0, 31, bit_body, prefix)
    thr_ref[...] = to_rows(thr)

    n_ge = count(lambda k, c: k >= thr[None])

    @pl.when(jnp.max(n_ge) > k_sel)
    def _():
        need = k_sel - count(lambda k, c: k > thr[None])
        pos_bits = max(seq - 1, 1).bit_length()
        pos = row.reshape(groups, SUBLANES, tile)

        def pos_body(it, x):
            cand = x | lax.shift_left(jnp.int32(1), pos_bits - 1 - it)
            f = count(lambda k, c: (k == thr[None]) & (pos + c * tile < cand[None]))
            return jnp.where(f < need, cand, x)

        last = lax.fori_loop(0, pos_bits, pos_body, jnp.zeros((SUBLANES, tile), I32))
        last_ref[...] = to_rows(last)
        reps = tile // LANES

        def drop_body(c, carry):
            cols = pl.ds(chunk_start(c), tile)
            k = keys_ref[:, cols]
            drop = (k == jnp.tile(thr_ref[...], (1, reps))) & (col + c * tile > jnp.tile(last_ref[...], (1, reps)))
            keys_ref[:, cols] = jnp.where(drop, INT_MIN, k)
            return carry

        lax.fori_loop(0, n_chunks, drop_body, 0)

    m_ref[...] = jnp.full(m_ref.shape, NEG, F32)
    l_ref[...] = jnp.zeros(l_ref.shape, F32)
    acc_ref[...] = jnp.zeros(acc_ref.shape, F32)
    q_all = qlat_ref[...].reshape(n_heads * tile, d_lat)
    sdiff = diff[0:strip, :]

    def attn_chunk(start, width, j):
        cols = pl.ds(start, width)
        c = ckv_ref[cols, :]
        s_ref[:, 0:width] = lax.dot_general(q_all, c, _NT, preferred_element_type=F32)
        for r0 in range(0, tile, strip):
            qrows = pl.ds(r0, strip)
            sel = keys_ref[qrows, cols] >= jnp.tile(thr_ref[qrows, :], (1, width // LANES))
            if j is not None:
                sel = sel & (sdiff - r0 <= j * tile)
            for h in range(n_heads):
                rows = pl.ds(h * tile + r0, strip)
                s = s_ref[rows, 0:width]
                if j is not None:
                    s = s + bias_ref[j, h, qrows, :]
                _softmax_strip(jnp.where(sel, s, NEG), rows, p_ref, alpha_ref, m_ref, l_ref)
        acc_ref[...] = alpha_ref[...] * acc_ref[...] + jnp.dot(p_ref[:, 0:width], c,
                                                                preferred_element_type=F32)

    far_width = FAR_TILES * tile

    def far_body(g, carry):
        attn_chunk(pl.multiple_of(g * far_width, far_width), far_width, None)
        return carry

    def near_body(kc, carry):
        attn_chunk(chunk_start(kc), tile, i - kc)
        return carry

    n_far_groups = jnp.maximum(i - 1, 0) // FAR_TILES
    lax.fori_loop(0, n_far_groups, far_body, 0)
    lax.fori_loop(n_far_groups * FAR_TILES, i + 1, near_body, 0)

    for h in range(n_heads):
        rows = slice(h * tile, (h + 1) * tile)
        l = jnp.sum(l_ref[rows, :], axis=1, keepdims=True)
        o_ref[:, h * d_lat:(h + 1) * d_lat] = (acc_ref[rows, :] / l).astype(BF16)


def _dsa(qlat, qi, wi, ki2, ckv, bias_tiles, *, k_sel, tile):
    n_heads, b, s, d_lat = qlat.shape
    assert d_lat == LANES
    w_idx = qi.shape[2]
    qblk = lambda bi, i: (bi, i, 0)
    whole = lambda bi, i: (bi, 0, 0)
    return pl.pallas_call(
        functools.partial(_dsa_kernel, tile=tile, k_sel=k_sel, seq=s, strip=min(ATTN_STRIP, tile)),
        grid=(b, s // tile),
        in_specs=[
            pl.BlockSpec((n_heads, None, tile, d_lat), lambda bi, i: (0, bi, i, 0)),
            pl.BlockSpec((None, tile, w_idx), qblk),
            pl.BlockSpec((None, tile, H_IDX), qblk),
            pl.BlockSpec((None, s, LANES), whole),
            pl.BlockSpec((None, s, d_lat), whole),
            _resident(bias_tiles.shape),
        ],
        out_specs=pl.BlockSpec((None, tile, n_heads * d_lat), qblk),
        out_shape=jax.ShapeDtypeStruct((b, s, n_heads * d_lat), BF16),
        scratch_shapes=[
            pltpu.VMEM((tile, s), I32),
            pltpu.VMEM((s, tile), I32),
            pltpu.VMEM((tile, LANES), I32),
            pltpu.VMEM((tile, LANES), I32),
            pltpu.VMEM((H_IDX * tile, LANES), BF16),
            pltpu.VMEM((H_IDX, tile, LANES), F32),
            pltpu.VMEM((max(n_heads, H_IDX) * tile, FAR_TILES * tile), F32),
            pltpu.VMEM((n_heads * tile, FAR_TILES * tile), BF16),
            pltpu.VMEM((n_heads * tile, LANES), F32),
            pltpu.VMEM((n_heads * tile, LANES), F32),
            pltpu.VMEM((n_heads * tile, LANES), F32),
            pltpu.VMEM((n_heads * tile, d_lat), F32),
        ],
        compiler_params=_params(("parallel", "arbitrary")),
        name="dsa",
    )(qlat, qi, wi, ki2, ckv, bias_tiles)


def _fox_kernel(q_ref, k_ref, v_ref, cq_ref, ck_ref, o_ref,
                s_ref, p_ref, cqr_ref, alpha_ref, m_ref, l_ref, acc_ref, *, tile, strip):
    n_pairs = q_ref.shape[0]
    i = pl.program_id(1)
    sdiff = (lax.broadcasted_iota(I32, (strip, tile), 1)
             - lax.broadcasted_iota(I32, (strip, tile), 0))
    lane = lax.broadcasted_iota(I32, (tile, LANES), 1)
    low_half = lane < HEAD_DIM
    zero = jnp.zeros((), BF16)
    q_all = []
    for pr in range(n_pairs):
        q = q_ref[pr]
        q_all.append(jnp.concatenate([jnp.where(low_half, q, zero), jnp.where(low_half, zero, q)], axis=0))
        for h in range(2):
            cqr_ref[(2 * pr + h) * tile:(2 * pr + h + 1) * tile, :] = jnp.broadcast_to(cq_ref[pr, h], (tile, LANES))
    m_ref[...] = jnp.full(m_ref.shape, NEG, F32)
    l_ref[...] = jnp.zeros(l_ref.shape, F32)
    acc_ref[...] = jnp.zeros(acc_ref.shape, F32)

    def chunk(start, width, j):
        cols = pl.ds(start, width)
        for pr in range(n_pairs):
            s_ref[pr * 2 * tile:(pr + 1) * 2 * tile, 0:width] = lax.dot_general(
                q_all[pr], k_ref[pr, cols, :], _NT, preferred_element_type=F32)
        ck = [ck_ref[pr, :, cols] for pr in range(n_pairs)]
        for r0 in range(0, tile, strip):
            for pr in range(n_pairs):
                for h in range(2):
                    rows = pl.ds((2 * pr + h) * tile + r0, strip)
                    s = s_ref[rows, 0:width] + jnp.tile(cqr_ref[rows, :], (1, width // LANES)) - ck[pr][h:h + 1, :]
                    if j is not None:
                        s = jnp.where(sdiff - r0 <= j * tile, s, NEG)
                    _softmax_strip(s, rows, p_ref, alpha_ref, m_ref, l_ref)
        for pr in range(n_pairs):
            rows = slice(pr * 2 * tile, (pr + 1) * 2 * tile)
            acc_ref[rows, :] = alpha_ref[rows, :] * acc_ref[rows, :] + jnp.dot(
                p_ref[rows, 0:width], v_ref[pr, cols, :], preferred_element_type=F32)

    far_width = FAR_TILES * tile

    def far_body(g, carry):
        chunk(pl.multiple_of(g * far_width, far_width), far_width, None)
        return carry

    def near_body(kc, carry):
        chunk(pl.multiple_of(kc * tile, tile), tile, i - kc)
        return carry

    n_far_groups = i // FAR_TILES
    lax.fori_loop(0, n_far_groups, far_body, 0)
    lax.fori_loop(n_far_groups * FAR_TILES, i + 1, near_body, 0)
    for pr in range(n_pairs):
        r0 = 2 * pr * tile
        o0 = acc_ref[r0:r0 + tile, :] / jnp.sum(l_ref[r0:r0 + tile, :], axis=1, keepdims=True)
        o1 = acc_ref[r0 + tile:r0 + 2 * tile, :] / jnp.sum(l_ref[r0 + tile:r0 + 2 * tile, :], axis=1, keepdims=True)
        o_ref[pr] = jnp.where(low_half, o0, o1).astype(BF16)


def _fox(qf, kf, vf, cum_col, cum_row, *, tile):
    npair, b, s, _ = qf.shape
    rows = 2 * npair * tile
    qblk = lambda bi, i: (0, bi, i, 0)
    whole = lambda bi, i: (0, bi, 0, 0)
    return pl.pallas_call(
        functools.partial(_fox_kernel, tile=tile, strip=min(ATTN_STRIP, tile)),
        grid=(b, s // tile),
        in_specs=[
            pl.BlockSpec((npair, None, tile, LANES), qblk),
            pl.BlockSpec((npair, None, s, LANES), whole),
            pl.BlockSpec((npair, None, s, LANES), whole),
            pl.BlockSpec((None, npair, 2, tile, 1), lambda bi, i: (bi, 0, 0, i, 0)),
            pl.BlockSpec((None, npair, 2, s), lambda bi, i: (bi, 0, 0, 0)),
        ],
        out_specs=pl.BlockSpec((npair, None, tile, LANES), qblk),
        out_shape=jax.ShapeDtypeStruct((npair, b, s, LANES), BF16),
        scratch_shapes=[
            pltpu.VMEM((rows, FAR_TILES * tile), F32),
            pltpu.VMEM((rows, FAR_TILES * tile), BF16),
            pltpu.VMEM((rows, LANES), F32),
            pltpu.VMEM((rows, LANES), F32),
            pltpu.VMEM((rows, LANES), F32),
            pltpu.VMEM((rows, LANES), F32),
            pltpu.VMEM((rows, LANES), F32),
        ],
        compiler_params=_params(("parallel", "arbitrary")),
        name="fox",
    )(qf, kf, vf, cum_col, cum_row)


def _mix_out_kernel(h_ref, ol_ref, of_ref, bdv_ref, wo_ref, o_ref):
    oa = jnp.dot(ol_ref[...], bdv_ref[...], preferred_element_type=F32).astype(BF16)
    w_dsa = oa.shape[1]
    y = jnp.dot(oa, wo_ref[0:w_dsa, :], preferred_element_type=F32)
    for hp in range(of_ref.shape[0]):
        r0 = w_dsa + hp * LANES
        y = y + jnp.dot(of_ref[hp], wo_ref[r0:r0 + LANES, :], preferred_element_type=F32)
    o_ref[...] = h_ref[...] + y


def _mix_out(h, olat, of, bdv, wo, *, tm=512):
    t, d = h.shape
    tm = min(tm, t)
    npair = of.shape[0]
    row = lambda i: (i, 0)
    return pl.pallas_call(
        _mix_out_kernel,
        grid=(t // tm,),
        in_specs=[
            pl.BlockSpec((tm, d), row),
            pl.BlockSpec((tm, olat.shape[1]), row),
            pl.BlockSpec((npair, tm, LANES), lambda i: (0, i, 0)),
            _resident(bdv.shape),
            _resident(wo.shape),
        ],
        out_specs=pl.BlockSpec((tm, d), row),
        out_shape=jax.ShapeDtypeStruct((t, d), F32),
        compiler_params=_params(("parallel",)),
        name="mix_out",
    )(h, olat, of, bdv, wo)


def _pad_cols(w, width):
    return jnp.pad(w, ((0, 0), (0, width - w.shape[1])))


def _mixer_weights(w_in, w_uk, w_uv, n_fox):
    d_lat, n_dsa, hd = w_uk.shape
    w_dsa = n_dsa * hd
    w_idx = H_IDX * D_IDX
    w_fox = n_fox * hd
    sizes = [w_dsa, d_lat, w_idx, D_IDX, H_IDX, 3 * w_fox, n_fox]
    assert sum(sizes) == w_in.shape[1] and d_lat == LANES and 2 * D_IDX == LANES
    offs = [0]
    for sz in sizes:
        offs.append(offs[-1] + sz)
    q_a, c_raw, q_i, k_i, w_i, qkv_f, f_l = [w_in[:, offs[j]:offs[j + 1]] for j in range(7)]
    w2 = jnp.concatenate(
        [q_a, c_raw, q_i, k_i, k_i, _pad_cols(w_i, LANES), _pad_cols(f_l, LANES), qkv_f], axis=1).astype(BF16)
    eye = jnp.eye(n_dsa, dtype=w_uk.dtype)
    bdk = jnp.einsum('chd,hg->hdgc', w_uk, eye).reshape(w_dsa, n_dsa * d_lat).astype(BF16)
    bdv = jnp.einsum('chd,hg->hcgd', w_uv, eye).reshape(n_dsa * d_lat, w_dsa).astype(BF16)
    return w2, bdk, bdv, dict(w_dsa=w_dsa, w_idx=w_idx, w_fox=w_fox, d_lat=d_lat)


def kernel(x, ffn1_norm, ffn1_gate, ffn1_up, ffn1_down, mix_norm, w_in, b_forget, ckv_norm, w_uk, w_uv, w_out,
           ffn2_norm, ffn2_gate, ffn2_up, ffn2_down, rel_bias, final_norm):
    b, s, d = x.shape
    depth = w_in.shape[0]
    n_fox = b_forget.shape[1]
    t = b * s
    tile = min(DSA_TILE, s)
    fox_tile = min(FOX_TILE, s)
    k_sel = min(TOPK_MAX, s // 4)
    npair = n_fox * HEAD_DIM // LANES

    bias_tiles = _bias_tiles(rel_bias, tile)
    fg = final_norm.reshape(1, d)
    h = x.reshape(t, d)
    for l in range(depth):
        h = _ffn(h, ffn1_norm[l].reshape(1, d), ffn1_gate[l].astype(BF16), ffn1_up[l].astype(BF16),
                 ffn1_down[l].astype(BF16), fg, final=False)

        w2, bdk, bdv, dims = _mixer_weights(w_in[l], w_uk[l], w_uv[l], n_fox)
        qlat, ckv, qi, ki2, wi, logf, qf, kf, vf = _mix_in(
            h, mix_norm[l].reshape(1, d), w2, ckv_norm[l].reshape(1, -1), bdk, b_forget[l].reshape(1, n_fox), **dims)

        cum_row = _cumsum_rows(jnp.swapaxes(logf.reshape(b, s, n_fox), 1, 2).reshape(b * n_fox, s))
        cum_row = cum_row.reshape(b, npair, 2, s)
        cum_col = cum_row[..., None]

        olat = _dsa(qlat.reshape(-1, b, s, LANES), qi.reshape(b, s, -1), wi.reshape(b, s, -1),
                    ki2.reshape(b, s, -1), ckv.reshape(b, s, -1), bias_tiles, k_sel=k_sel, tile=tile)
        of = _fox(qf.reshape(npair, b, s, LANES), kf.reshape(npair, b, s, LANES), vf.reshape(npair, b, s, LANES),
                  cum_col, cum_row, tile=fox_tile)

        h = _mix_out(h, olat.reshape(t, -1), of.reshape(npair, t, LANES), bdv, w_out[l].astype(BF16))
        h = _ffn(h, ffn2_norm[l].reshape(1, d), ffn2_gate[l].astype(BF16), ffn2_up[l].astype(BF16),
                 ffn2_down[l].astype(BF16), fg, final=(l == depth - 1))
    return h.reshape(b, s, d)
```

```python
import functools
import math

import jax
import jax.numpy as jnp
from jax import lax
from jax.experimental import pallas as pl
from jax.experimental.pallas import tpu as pltpu

HEAD_DIM = 64
H_IDX = 8
D_IDX = 64
TOPK_MAX = 256
MAX_DISTANCE = 128
EPS = 1e-6
NEG = -1e30

LANES = 128
SUBLANES = 8
V7X_VMEM_LIMIT = 56 * 1024 * 1024

DSA_TILE = 256
FOX_TILE = 256
ATTN_STRIP = 32
FAR_TILES = 2
N_BIAS_TILES = 3

LOG2E = math.log2(math.e)
INT_MIN = -(2 ** 31)
F32 = jnp.float32
BF16 = jnp.bfloat16
I32 = jnp.int32

_NT = (((1,), (1,)), ((), ()))


def _params(sem):
    return pltpu.CompilerParams(dimension_semantics=sem, vmem_limit_bytes=V7X_VMEM_LIMIT)


def _resident(shape):
    zeros = (0,) * len(shape)
    return pl.BlockSpec(shape, lambda *_: zeros, pipeline_mode=pl.Buffered(1))


def _rms(x, g):
    y = x * lax.rsqrt(jnp.mean(x * x, axis=-1, keepdims=True) + EPS)
    return y * g


def _swiglu_half_step(x, g_ref, wg_ref, wu_ref, wd_ref, acc_ref, ff_chunk):
    xn = _rms(x, g_ref[...]).astype(BF16)
    d_ff = wg_ref.shape[1]
    for c0 in range(0, d_ff, ff_chunk):
        c1 = min(c0 + ff_chunk, d_ff)
        g = jnp.dot(xn, wg_ref[:, c0:c1], preferred_element_type=F32)
        u = jnp.dot(xn, wu_ref[:, c0:c1], preferred_element_type=F32)
        a = (g * jax.nn.sigmoid(g) * u).astype(BF16)
        d = jnp.dot(a, wd_ref[c0:c1, :], preferred_element_type=F32)
        if c0 == 0:
            acc_ref[...] = d
        else:
            acc_ref[...] += d
    return x + 0.5 * acc_ref[...]


def _ffn_kernel(x_ref, g_ref, wg_ref, wu_ref, wd_ref, o_ref, acc_ref, *, ff_chunk):
    o_ref[...] = _swiglu_half_step(x_ref[...], g_ref, wg_ref, wu_ref, wd_ref, acc_ref, ff_chunk)


def _ffn(h, g, wg, wu, wd, *, tm=512, ff_chunk=256):
    t, d = h.shape
    d_ff = wg.shape[1]
    tm = min(tm, t)
    return pl.pallas_call(
        functools.partial(_ffn_kernel, ff_chunk=ff_chunk),
        grid=(t // tm,),
        in_specs=[
            pl.BlockSpec((tm, d), lambda i: (i, 0)),
            _resident((1, d)),
            _resident((d, d_ff)),
            _resident((d, d_ff)),
            _resident((d_ff, d)),
        ],
        out_specs=pl.BlockSpec((tm, d), lambda i: (i, 0)),
        out_shape=jax.ShapeDtypeStruct((t, d), F32),
        scratch_shapes=[pltpu.VMEM((tm, d), F32)],
        compiler_params=_params(("parallel",)),
        name="ffn",
    )(h, g, wg, wu, wd)


def _mix_in_kernel(x_ref, g_ref, w_ref, gc_ref, bdk_ref, bf_ref,
                   qlat_ref, ckv_ref, qi_ref, ki_ref, wi_ref, lf_ref, qf_ref, kf_ref, vf_ref,
                   *, w_dsa, w_idx, n_idx_heads, w_fox):
    xn = _rms(x_ref[...], g_ref[...]).astype(BF16)
    pos = [0]

    def proj(width):
        c0 = pos[0]
        pos[0] = c0 + width
        return jnp.dot(xn, w_ref[:, c0:c0 + width], preferred_element_type=F32)

    qk_scale = HEAD_DIM ** -0.5 * LOG2E
    idx_scale = D_IDX ** -0.5
    qa = proj(w_dsa).astype(BF16)
    qlat = jnp.dot(qa, bdk_ref[...], preferred_element_type=F32)
    for h in range(qlat_ref.shape[0]):
        qlat_ref[h] = (qlat[:, h * LANES:(h + 1) * LANES] * qk_scale).astype(BF16)
    ckv_ref[...] = _rms(proj(LANES), gc_ref[...]).astype(BF16)
    qi_ref[...] = (proj(w_idx) * idx_scale).astype(BF16)
    ki_ref[...] = proj(LANES).astype(BF16)
    wi_ref[...] = proj(LANES)[:, :n_idx_heads] * (n_idx_heads ** -0.5)
    n_fox = bf_ref.shape[1]
    f = proj(LANES)[:, :n_fox] + bf_ref[...]
    lf_ref[...] = jnp.minimum(f, 0.0) - jnp.log1p(jnp.exp(-jnp.abs(f)))
    for hp in range(w_fox // LANES):
        qf_ref[hp] = (proj(LANES) * qk_scale).astype(BF16)
    for hp in range(w_fox // LANES):
        kf_ref[hp] = proj(LANES).astype(BF16)
    for hp in range(w_fox // LANES):
        vf_ref[hp] = proj(LANES).astype(BF16)


def _mix_in(h, g, w2, gc, bdk, bfg, *, w_dsa, w_idx, w_fox, d_lat, tm=512):
    t, d = h.shape
    tm = min(tm, t)
    n_dsa = w_dsa // HEAD_DIM
    n_fox = bfg.shape[1]
    npair = w_fox // LANES
    row = lambda i: (i, 0)
    row3 = lambda i: (0, i, 0)
    out_shape = [
        jax.ShapeDtypeStruct((n_dsa, t, d_lat), BF16),
        jax.ShapeDtypeStruct((t, d_lat), BF16),
        jax.ShapeDtypeStruct((t, w_idx), BF16),
        jax.ShapeDtypeStruct((t, LANES), BF16),
        jax.ShapeDtypeStruct((t, H_IDX), F32),
        jax.ShapeDtypeStruct((t, n_fox), F32),
        jax.ShapeDtypeStruct((npair, t, LANES), BF16),
        jax.ShapeDtypeStruct((npair, t, LANES), BF16),
        jax.ShapeDtypeStruct((npair, t, LANES), BF16),
    ]
    out_specs = [
        pl.BlockSpec((n_dsa, tm, d_lat), row3),
        pl.BlockSpec((tm, d_lat), row),
        pl.BlockSpec((tm, w_idx), row),
        pl.BlockSpec((tm, LANES), row),
        pl.BlockSpec((tm, H_IDX), row),
        pl.BlockSpec((tm, n_fox), row),
        pl.BlockSpec((npair, tm, LANES), row3),
        pl.BlockSpec((npair, tm, LANES), row3),
        pl.BlockSpec((npair, tm, LANES), row3),
    ]
    return pl.pallas_call(
        functools.partial(_mix_in_kernel, w_dsa=w_dsa, w_idx=w_idx, n_idx_heads=H_IDX, w_fox=w_fox),
        grid=(t // tm,),
        in_specs=[
            pl.BlockSpec((tm, d), row),
            _resident((1, d)),
            _resident(w2.shape),
            _resident((1, d_lat)),
            _resident(bdk.shape),
            _resident((1, n_fox)),
        ],
        out_specs=out_specs,
        out_shape=out_shape,
        compiler_params=_params(("parallel",)),
        name="mix_in",
    )(h, g, w2, gc, bdk, bfg)


def _cumsum_kernel(x_ref, o_ref):
    x = x_ref[...]
    n = x.shape[1]
    lane = lax.broadcasted_iota(I32, x.shape, 1)
    sh = 1
    while sh < n:
        x = x + jnp.where(lane >= sh, pltpu.roll(x, sh, axis=1), 0.0)
        sh *= 2
    o_ref[...] = x * LOG2E


def _cumsum_rows(x):
    r, n = x.shape
    rb = 8 if r % 8 == 0 else r
    return pl.pallas_call(
        _cumsum_kernel,
        grid=(r // rb,),
        in_specs=[pl.BlockSpec((rb, n), lambda i: (i, 0))],
        out_specs=pl.BlockSpec((rb, n), lambda i: (i, 0)),
        out_shape=jax.ShapeDtypeStruct((r, n), F32),
        compiler_params=_params(("parallel",)),
        name="cumsum",
    )(x)


def _bias_tiles_kernel(rb_ref, bucket_ref, o_ref):
    n_buckets, n_heads = rb_ref.shape
    for j in range(bucket_ref.shape[0]):
        bucket = bucket_ref[j]
        for h in range(n_heads):
            acc = jnp.zeros(bucket.shape, F32)
            for b in range(n_buckets):
                acc = jnp.where(bucket == b, rb_ref[b, h], acc)
            o_ref[j, h] = (acc - rb_ref[n_buckets - 1, h]) * LOG2E


def _t5_bucket(dist, n_buckets):
    n = jnp.maximum(dist, 0)
    max_exact = n_buckets // 2
    nf = jnp.maximum(n, 1).astype(F32)
    large = max_exact + (jnp.log(nf / max_exact) / math.log(MAX_DISTANCE / max_exact)
                         * (n_buckets - max_exact)).astype(I32)
    large = jnp.minimum(large, n_buckets - 1)
    return jnp.where(n < max_exact, n, large)


def _bias_tiles(rel_bias, tile):
    n_buckets, n_heads = rel_bias.shape
    assert tile >= MAX_DISTANCE
    assert N_BIAS_TILES == FAR_TILES + 1
    r = lax.broadcasted_iota(I32, (tile, tile), 0)
    c = lax.broadcasted_iota(I32, (tile, tile), 1)
    buckets = jnp.stack([_t5_bucket(j * tile + r - c, n_buckets) for j in range(N_BIAS_TILES)])
    return pl.pallas_call(
        _bias_tiles_kernel,
        in_specs=[pl.BlockSpec(memory_space=pltpu.SMEM), pl.BlockSpec(memory_space=pltpu.VMEM)],
        out_specs=pl.BlockSpec(memory_space=pltpu.VMEM),
        out_shape=jax.ShapeDtypeStruct((N_BIAS_TILES, n_heads, tile, tile), F32),
        compiler_params=pltpu.CompilerParams(vmem_limit_bytes=V7X_VMEM_LIMIT),
        name="bias_tiles",
    )(rel_bias, buckets)


def _softmax_strip(s, rows, p_ref, alpha_ref, m_ref, l_ref):
    width = s.shape[1]
    reps = width // LANES
    m_old = m_ref[rows, :]
    m_new = jnp.maximum(m_old, jnp.max(s, axis=1, keepdims=True))
    alpha = jnp.exp2(m_old - m_new)
    p = jnp.exp2(s - jnp.tile(m_new, (1, reps)))
    psum = p[:, 0:LANES]
    for j in range(1, reps):
        psum = psum + p[:, j * LANES:(j + 1) * LANES]
    l_ref[rows, :] = alpha * l_ref[rows, :] + psum
    alpha_ref[rows, :] = alpha
    m_ref[rows, :] = m_new
    p_ref[rows, 0:width] = p.astype(BF16)


def _dsa_kernel(qlat_ref, qi_ref, wi_ref, ki_ref, ckv_ref, bias_ref, o_ref,
                keys_ref, keyst_ref, thr_ref, last_ref, qidx_ref, widx_ref,
                s_ref, p_ref, alpha_ref, m_ref, l_ref, acc_ref,
                *, tile, k_sel, seq, strip):
    n_heads, _, d_lat = qlat_ref.shape
    i = pl.program_id(1)
    n_chunks = i + 1
    row = lax.broadcasted_iota(I32, (tile, tile), 0)
    col = lax.broadcasted_iota(I32, (tile, tile), 1)
    diff = col - row
    lane = lax.broadcasted_iota(I32, (tile, LANES), 1)
    low_half = lane < D_IDX

    def chunk_start(kc):
        return pl.multiple_of(kc * tile, tile)

    qi = qi_ref[...]
    wi = wi_ref[...]
    zero = jnp.zeros((), BF16)
    for h in range(H_IDX):
        grp = qi[:, (h // 2) * LANES:(h // 2 + 1) * LANES]
        qidx_ref[h * tile:(h + 1) * tile, :] = (jnp.where(low_half, grp, zero) if h % 2 == 0
                                                 else jnp.where(low_half, zero, grp))
        widx_ref[h] = jnp.broadcast_to(wi[:, h:h + 1], (tile, LANES))

    def score_chunk(start, width, j):
        k2 = ki_ref[pl.ds(start, width), :]
        acc = None
        for h in range(H_IDX):
            r = lax.dot_general(qidx_ref[h * tile:(h + 1) * tile, :], k2, _NT, preferred_element_type=F32)
            term = jnp.maximum(r, 0.0) * jnp.tile(widx_ref[h], (1, width // LANES))
            acc = term if acc is None else acc + term
        if j is not None:
            acc = jnp.where(diff <= j * tile, acc, NEG)
        bits = pltpu.bitcast(acc, I32)
        key = jnp.where(bits < 0, INT_MIN - bits, bits)
        keys_ref[:, pl.ds(start, width)] = key
        keyst_ref[pl.ds(start, width), :] = key.T

    score_width = FAR_TILES * tile

    def score_far(g, carry):
        score_chunk(pl.multiple_of(g * score_width, score_width), score_width, None)
        return carry

    def score_near(kc, carry):
        score_chunk(chunk_start(kc), tile, i - kc)
        return carry

    n_score_groups = i // FAR_TILES
    lax.fori_loop(0, n_score_groups, score_far, 0)
    lax.fori_loop(n_score_groups * FAR_TILES, i + 1, score_near, 0)

    groups = tile // SUBLANES

    def key_tile(c):
        return keyst_ref[pl.ds(chunk_start(c), tile), :].reshape(groups, SUBLANES, tile)

    def count(pred):
        def one(c, cnt):
            return cnt + jnp.sum(pred(key_tile(c), c).astype(I32), axis=0)

        def two(g, cnt):
            return one(2 * g + 1, one(2 * g, cnt))

        cnt = lax.fori_loop(0, n_chunks // 2, two, jnp.zeros((SUBLANES, tile), I32))
        cnt = lax.fori_loop(2 * (n_chunks // 2), n_chunks, one, cnt)
        return jnp.broadcast_to(jnp.sum(cnt, axis=0, keepdims=True), (SUBLANES, tile))

    def to_rows(x):
        return jnp.broadcast_to(x[0:1, :], (LANES, tile)).T

    n_nonneg = count(lambda k, c: k >= 0)
    start_high = n_nonneg >= k_sel
    prefix = jnp.where(start_high, 0, INT_MIN).astype(I32)
    n_ge = jnp.where(start_high, n_nonneg, n_chunks * tile)

    def bit_body(it, state):
        prefix, n_ge = state
        cand = prefix | lax.shift_left(jnp.int32(1), 30 - it)
        cnt = count(lambda k, c: k >= cand[None])
        keep = cnt >= k_sel
        return jnp.where(keep, cand, prefix), jnp.where(keep, cnt, n_ge)

    thr, n_ge = lax.fori_loop(0, 31, bit_body, (prefix, n_ge))
    thr_ref[...] = to_rows(thr)


    @pl.when(jnp.max(n_ge) > k_sel)
    def _():
        need = k_sel - count(lambda k, c: k > thr[None])
        pos_bits = max(seq - 1, 1).bit_length()
        pos = row.reshape(groups, SUBLANES, tile)

        def pos_body(it, x):
            cand = x | lax.shift_left(jnp.int32(1), pos_bits - 1 - it)
            f = count(lambda k, c: (k == thr[None]) & (pos + c * tile < cand[None]))
            return jnp.where(f < need, cand, x)

        last = lax.fori_loop(0, pos_bits, pos_body, jnp.zeros((SUBLANES, tile), I32))
        last_ref[...] = to_rows(last)
        reps = tile // LANES

        def drop_body(c, carry):
            cols = pl.ds(chunk_start(c), tile)
            k = keys_ref[:, cols]
            drop = (k == jnp.tile(thr_ref[...], (1, reps))) & (col + c * tile > jnp.tile(last_ref[...], (1, reps)))
            keys_ref[:, cols] = jnp.where(drop, INT_MIN, k)
            return carry

        lax.fori_loop(0, n_chunks, drop_body, 0)

    m_ref[...] = jnp.full(m_ref.shape, NEG, F32)
    l_ref[...] = jnp.zeros(l_ref.shape, F32)
    acc_ref[...] = jnp.zeros(acc_ref.shape, F32)
    q_all = qlat_ref[...].reshape(n_heads * tile, d_lat)
    sdiff = diff[0:strip, :]

    def attn_chunk(start, width, j):
        cols = pl.ds(start, width)
        c = ckv_ref[cols, :]
        s_ref[:, 0:width] = lax.dot_general(q_all, c, _NT, preferred_element_type=F32)
        for r0 in range(0, tile, strip):
            qrows = pl.ds(r0, strip)
            sel = keys_ref[qrows, cols] >= jnp.tile(thr_ref[qrows, :], (1, width // LANES))
            if j is not None:
                sel = sel & (sdiff - r0 <= j * tile)
            penalty = jnp.where(sel, 0.0, NEG)
            for h in range(n_heads):
                rows = pl.ds(h * tile + r0, strip)
                s = s_ref[rows, 0:width] + penalty
                if j is not None:
                    s = s + bias_ref[j, h, qrows, :]
                _softmax_strip(s, rows, p_ref, alpha_ref, m_ref, l_ref)
        acc_ref[...] = alpha_ref[...] * acc_ref[...] + jnp.dot(p_ref[:, 0:width], c,
                                                                preferred_element_type=F32)

    far_width = FAR_TILES * tile

    def far_body(g, carry):
        attn_chunk(pl.multiple_of(g * far_width, far_width), far_width, None)
        return carry

    def near_body(kc, carry):
        attn_chunk(chunk_start(kc), tile, i - kc)
        return carry

    n_far_groups = jnp.maximum(i - 1, 0) // FAR_TILES
    lax.fori_loop(0, n_far_groups, far_body, 0)
    lax.fori_loop(n_far_groups * FAR_TILES, i + 1, near_body, 0)

    for h in range(n_heads):
        rows = slice(h * tile, (h + 1) * tile)
        l = jnp.sum(l_ref[rows, :], axis=1, keepdims=True)
        o_ref[:, h * d_lat:(h + 1) * d_lat] = (acc_ref[rows, :] / l).astype(BF16)


def _dsa(qlat, qi, wi, ki2, ckv, bias_tiles, *, k_sel, tile):
    n_heads, b, s, d_lat = qlat.shape
    assert d_lat == LANES
    w_idx = qi.shape[2]
    qblk = lambda bi, i: (bi, i, 0)
    whole = lambda bi, i: (bi, 0, 0)
    return pl.pallas_call(
        functools.partial(_dsa_kernel, tile=tile, k_sel=k_sel, seq=s, strip=min(ATTN_STRIP, tile)),
        grid=(b, s // tile),
        in_specs=[
            pl.BlockSpec((n_heads, None, tile, d_lat), lambda bi, i: (0, bi, i, 0)),
            pl.BlockSpec((None, tile, w_idx), qblk),
            pl.BlockSpec((None, tile, H_IDX), qblk),
            pl.BlockSpec((None, s, LANES), whole),
            pl.BlockSpec((None, s, d_lat), whole),
            _resident(bias_tiles.shape),
        ],
        out_specs=pl.BlockSpec((None, tile, n_heads * d_lat), qblk),
        out_shape=jax.ShapeDtypeStruct((b, s, n_heads * d_lat), BF16),
        scratch_shapes=[
            pltpu.VMEM((tile, s), I32),
            pltpu.VMEM((s, tile), I32),
            pltpu.VMEM((tile, LANES), I32),
            pltpu.VMEM((tile, LANES), I32),
            pltpu.VMEM((H_IDX * tile, LANES), BF16),
            pltpu.VMEM((H_IDX, tile, LANES), F32),
            pltpu.VMEM((n_heads * tile, FAR_TILES * tile), F32),
            pltpu.VMEM((n_heads * tile, FAR_TILES * tile), BF16),
            pltpu.VMEM((n_heads * tile, LANES), F32),
            pltpu.VMEM((n_heads * tile, LANES), F32),
            pltpu.VMEM((n_heads * tile, LANES), F32),
            pltpu.VMEM((n_heads * tile, d_lat), F32),
        ],
        compiler_params=_params(("parallel", "arbitrary")),
        name="dsa",
    )(qlat, qi, wi, ki2, ckv, bias_tiles)


def _fox_kernel(q_ref, k_ref, v_ref, cq_ref, ck_ref, o_ref,
                s_ref, p_ref, cqr_ref, alpha_ref, m_ref, l_ref, acc_ref, *, tile, strip):
    n_pairs = q_ref.shape[0]
    i = pl.program_id(1)
    sdiff = (lax.broadcasted_iota(I32, (strip, tile), 1)
             - lax.broadcasted_iota(I32, (strip, tile), 0))
    lane = lax.broadcasted_iota(I32, (tile, LANES), 1)
    low_half = lane < HEAD_DIM
    zero = jnp.zeros((), BF16)
    q_all = []
    for pr in range(n_pairs):
        q = q_ref[pr]
        q_all.append(jnp.concatenate([jnp.where(low_half, q, zero), jnp.where(low_half, zero, q)], axis=0))
        for h in range(2):
            cqr_ref[(2 * pr + h) * tile:(2 * pr + h + 1) * tile, :] = jnp.broadcast_to(cq_ref[pr, h], (tile, LANES))
    m_ref[...] = jnp.full(m_ref.shape, NEG, F32)
    l_ref[...] = jnp.zeros(l_ref.shape, F32)
    acc_ref[...] = jnp.zeros(acc_ref.shape, F32)

    def chunk(start, width, j):
        cols = pl.ds(start, width)
        for pr in range(n_pairs):
            s_ref[pr * 2 * tile:(pr + 1) * 2 * tile, 0:width] = lax.dot_general(
                q_all[pr], k_ref[pr, cols, :], _NT, preferred_element_type=F32)
        ck = [ck_ref[pr, :, cols] for pr in range(n_pairs)]
        for r0 in range(0, tile, strip):
            if j is not None:
                penalty = jnp.where(sdiff - r0 <= j * tile, 0.0, NEG)
            for pr in range(n_pairs):
                for h in range(2):
                    rows = pl.ds((2 * pr + h) * tile + r0, strip)
                    s = s_ref[rows, 0:width] + jnp.tile(cqr_ref[rows, :], (1, width // LANES)) - ck[pr][h:h + 1, :]
                    if j is not None:
                        s = s + penalty
                    _softmax_strip(s, rows, p_ref, alpha_ref, m_ref, l_ref)
        for pr in range(n_pairs):
            rows = slice(pr * 2 * tile, (pr + 1) * 2 * tile)
            acc_ref[rows, :] = alpha_ref[rows, :] * acc_ref[rows, :] + jnp.dot(
                p_ref[rows, 0:width], v_ref[pr, cols, :], preferred_element_type=F32)

    far_width = FAR_TILES * tile

    def far_body(g, carry):
        chunk(pl.multiple_of(g * far_width, far_width), far_width, None)
        return carry

    def near_body(kc, carry):
        chunk(pl.multiple_of(kc * tile, tile), tile, i - kc)
        return carry

    n_far_groups = i // FAR_TILES
    lax.fori_loop(0, n_far_groups, far_body, 0)
    lax.fori_loop(n_far_groups * FAR_TILES, i + 1, near_body, 0)
    for pr in range(n_pairs):
        r0 = 2 * pr * tile
        o0 = acc_ref[r0:r0 + tile, :] / jnp.sum(l_ref[r0:r0 + tile, :], axis=1, keepdims=True)
        o1 = acc_ref[r0 + tile:r0 + 2 * tile, :] / jnp.sum(l_ref[r0 + tile:r0 + 2 * tile, :], axis=1, keepdims=True)
        o_ref[pr] = jnp.where(low_half, o0, o1).astype(BF16)


def _fox(qf, kf, vf, cum_col, cum_row, *, tile):
    npair, b, s, _ = qf.shape
    rows = 2 * npair * tile
    qblk = lambda bi, i: (0, bi, i, 0)
    whole = lambda bi, i: (0, bi, 0, 0)
    return pl.pallas_call(
        functools.partial(_fox_kernel, tile=tile, strip=min(ATTN_STRIP, tile)),
        grid=(b, s // tile),
        in_specs=[
            pl.BlockSpec((npair, None, tile, LANES), qblk),
            pl.BlockSpec((npair, None, s, LANES), whole),
            pl.BlockSpec((npair, None, s, LANES), whole),
            pl.BlockSpec((None, npair, 2, tile, 1), lambda bi, i: (bi, 0, 0, i, 0)),
            pl.BlockSpec((None, npair, 2, s), lambda bi, i: (bi, 0, 0, 0)),
        ],
        out_specs=pl.BlockSpec((npair, None, tile, LANES), qblk),
        out_shape=jax.ShapeDtypeStruct((npair, b, s, LANES), BF16),
        scratch_shapes=[
            pltpu.VMEM((rows, FAR_TILES * tile), F32),
            pltpu.VMEM((rows, FAR_TILES * tile), BF16),
            pltpu.VMEM((rows, LANES), F32),
            pltpu.VMEM((rows, LANES), F32),
            pltpu.VMEM((rows, LANES), F32),
            pltpu.VMEM((rows, LANES), F32),
            pltpu.VMEM((rows, LANES), F32),
        ],
        compiler_params=_params(("parallel", "arbitrary")),
        name="fox",
    )(qf, kf, vf, cum_col, cum_row)


def _mix_out_ffn_kernel(h_ref, ol_ref, of_ref, bdv_ref, wo_ref, g_ref, wg_ref, wu_ref, wd_ref, fg_ref,
                        o_ref, acc_ref, *, ff_chunk, final):
    oa = jnp.dot(ol_ref[...], bdv_ref[...], preferred_element_type=F32).astype(BF16)
    w_dsa = oa.shape[1]
    y = jnp.dot(oa, wo_ref[0:w_dsa, :], preferred_element_type=F32)
    for hp in range(of_ref.shape[0]):
        r0 = w_dsa + hp * LANES
        y = y + jnp.dot(of_ref[hp], wo_ref[r0:r0 + LANES, :], preferred_element_type=F32)
    out = _swiglu_half_step(h_ref[...] + y, g_ref, wg_ref, wu_ref, wd_ref, acc_ref, ff_chunk)
    if final:
        out = _rms(out, fg_ref[...])
    o_ref[...] = out


def _mix_out_ffn(h, olat, of, bdv, wo, g, wg, wu, wd, fg, *, final, tm=512, ff_chunk=256):
    t, d = h.shape
    d_ff = wg.shape[1]
    tm = min(tm, t)
    npair = of.shape[0]
    row = lambda i: (i, 0)
    return pl.pallas_call(
        functools.partial(_mix_out_ffn_kernel, ff_chunk=ff_chunk, final=final),
        grid=(t // tm,),
        in_specs=[
            pl.BlockSpec((tm, d), row),
            pl.BlockSpec((tm, olat.shape[1]), row),
            pl.BlockSpec((npair, tm, LANES), lambda i: (0, i, 0)),
            _resident(bdv.shape),
            _resident(wo.shape),
            _resident((1, d)),
            _resident((d, d_ff)),
            _resident((d, d_ff)),
            _resident((d_ff, d)),
            _resident((1, d)),
        ],
        out_specs=pl.BlockSpec((tm, d), row),
        out_shape=jax.ShapeDtypeStruct((t, d), F32),
        scratch_shapes=[pltpu.VMEM((tm, d), F32)],
        compiler_params=_params(("parallel",)),
        name="mix_out_ffn",
    )(h, olat, of, bdv, wo, g, wg, wu, wd, fg)


def _pad_cols(w, width):
    return jnp.pad(w, ((0, 0), (0, width - w.shape[1])))


def _mixer_weights(w_in, w_uk, w_uv, n_fox):
    d_lat, n_dsa, hd = w_uk.shape
    w_dsa = n_dsa * hd
    w_idx = H_IDX * D_IDX
    w_fox = n_fox * hd
    sizes = [w_dsa, d_lat, w_idx, D_IDX, H_IDX, 3 * w_fox, n_fox]
    assert sum(sizes) == w_in.shape[1] and d_lat == LANES and 2 * D_IDX == LANES
    offs = [0]
    for sz in sizes:
        offs.append(offs[-1] + sz)
    q_a, c_raw, q_i, k_i, w_i, qkv_f, f_l = [w_in[:, offs[j]:offs[j + 1]] for j in range(7)]
    w2 = jnp.concatenate(
        [q_a, c_raw, q_i, k_i, k_i, _pad_cols(w_i, LANES), _pad_cols(f_l, LANES), qkv_f], axis=1).astype(BF16)
    eye = jnp.eye(n_dsa, dtype=w_uk.dtype)
    bdk = jnp.einsum('chd,hg->hdgc', w_uk, eye).reshape(w_dsa, n_dsa * d_lat).astype(BF16)
    bdv = jnp.einsum('chd,hg->hcgd', w_uv, eye).reshape(n_dsa * d_lat, w_dsa).astype(BF16)
    return w2, bdk, bdv, dict(w_dsa=w_dsa, w_idx=w_idx, w_fox=w_fox, d_lat=d_lat)


def kernel(x, ffn1_norm, ffn1_gate, ffn1_up, ffn1_down, mix_norm, w_in, b_forget, ckv_norm, w_uk, w_uv, w_out,
           ffn2_norm, ffn2_gate, ffn2_up, ffn2_down, rel_bias, final_norm):
    b, s, d = x.shape
    depth = w_in.shape[0]
    n_fox = b_forget.shape[1]
    t = b * s
    tile = min(DSA_TILE, s)
    fox_tile = min(FOX_TILE, s)
    k_sel = min(TOPK_MAX, s // 4)
    npair = n_fox * HEAD_DIM // LANES

    bias_tiles = _bias_tiles(rel_bias, tile)
    fg = final_norm.reshape(1, d)
    h = x.reshape(t, d)
    for l in range(depth):
        h = _ffn(h, ffn1_norm[l].reshape(1, d), ffn1_gate[l].astype(BF16), ffn1_up[l].astype(BF16),
                 ffn1_down[l].astype(BF16))

        w2, bdk, bdv, dims = _mixer_weights(w_in[l], w_uk[l], w_uv[l], n_fox)
        qlat, ckv, qi, ki2, wi, logf, qf, kf, vf = _mix_in(
            h, mix_norm[l].reshape(1, d), w2, ckv_norm[l].reshape(1, -1), bdk, b_forget[l].reshape(1, n_fox), **dims)

        cum_row = _cumsum_rows(jnp.swapaxes(logf.reshape(b, s, n_fox), 1, 2).reshape(b * n_fox, s))
        cum_row = cum_row.reshape(b, npair, 2, s)
        cum_col = cum_row[..., None]

        olat = _dsa(qlat.reshape(-1, b, s, LANES), qi.reshape(b, s, -1), wi.reshape(b, s, -1),
                    ki2.reshape(b, s, -1), ckv.reshape(b, s, -1), bias_tiles, k_sel=k_sel, tile=tile)
        of = _fox(qf.reshape(npair, b, s, LANES), kf.reshape(npair, b, s, LANES), vf.reshape(npair, b, s, LANES),
                  cum_col, cum_row, tile=fox_tile)

        h = _mix_out_ffn(h, olat.reshape(t, -1), of.reshape(npair, t, LANES), bdv, w_out[l].astype(BF16),
                         ffn2_norm[l].reshape(1, d), ffn2_gate[l].astype(BF16), ffn2_up[l].astype(BF16),
                         ffn2_down[l].astype(BF16), fg, final=(l == depth - 1))
    return h.reshape(b, s, d)
```

```python
import functools
import math

import jax
import jax.numpy as jnp
from jax import lax
from jax.experimental import pallas as pl
from jax.experimental.pallas import tpu as pltpu

HEAD_DIM = 64
H_IDX = 8
D_IDX = 64
TOPK_MAX = 256
MAX_DISTANCE = 128
EPS = 1e-6
NEG = -1e30

LANES = 128
SUBLANES = 8
V7X_VMEM_LIMIT = 56 * 1024 * 1024

DSA_TILE = 256
FOX_TILE = 256
ATTN_STRIP = 32
FAR_TILES = 2
N_BIAS_TILES = 3

LOG2E = math.log2(math.e)
INT_MIN = -(2 ** 31)
F32 = jnp.float32
BF16 = jnp.bfloat16
I32 = jnp.int32

_NT = (((1,), (1,)), ((), ()))


def _params(sem):
    return pltpu.CompilerParams(dimension_semantics=sem, vmem_limit_bytes=V7X_VMEM_LIMIT)


def _resident(shape):
    zeros = (0,) * len(shape)
    return pl.BlockSpec(shape, lambda *_: zeros, pipeline_mode=pl.Buffered(1))


def _layer(stacked, l):
    _, r, c = stacked.shape
    return pl.BlockSpec((None, r, c), lambda *_: (l, 0, 0), pipeline_mode=pl.Buffered(1))


def _rms(x, g):
    y = x * lax.rsqrt(jnp.mean(x * x, axis=-1, keepdims=True) + EPS)
    return y * g


def _swiglu_half_step(x, g_ref, wg_ref, wu_ref, wd_ref, acc_ref, ff_chunk):
    xn = _rms(x, g_ref[...]).astype(BF16)
    d_ff = wg_ref.shape[1]
    for c0 in range(0, d_ff, ff_chunk):
        c1 = min(c0 + ff_chunk, d_ff)
        g = jnp.dot(xn, wg_ref[:, c0:c1], preferred_element_type=F32)
        u = jnp.dot(xn, wu_ref[:, c0:c1], preferred_element_type=F32)
        a = (g * jax.nn.sigmoid(g) * u).astype(BF16)
        d = jnp.dot(a, wd_ref[c0:c1, :], preferred_element_type=F32)
        if c0 == 0:
            acc_ref[...] = d
        else:
            acc_ref[...] += d
    return x + 0.5 * acc_ref[...]


def _ffn_kernel(x_ref, g_ref, wg_ref, wu_ref, wd_ref, o_ref, acc_ref, *, ff_chunk):
    o_ref[...] = _swiglu_half_step(x_ref[...], g_ref, wg_ref, wu_ref, wd_ref, acc_ref, ff_chunk)


def _ffn(h, g, wg, wu, wd, l, *, tm=512, ff_chunk=256):
    t, d = h.shape
    tm = min(tm, t)
    return pl.pallas_call(
        functools.partial(_ffn_kernel, ff_chunk=ff_chunk),
        grid=(t // tm,),
        in_specs=[
            pl.BlockSpec((tm, d), lambda i: (i, 0)),
            _layer(g, l),
            _layer(wg, l),
            _layer(wu, l),
            _layer(wd, l),
        ],
        out_specs=pl.BlockSpec((tm, d), lambda i: (i, 0)),
        out_shape=jax.ShapeDtypeStruct((t, d), F32),
        scratch_shapes=[pltpu.VMEM((tm, d), F32)],
        compiler_params=_params(("parallel",)),
        name="ffn",
    )(h, g, wg, wu, wd)


def _mix_in_kernel(x_ref, g_ref, w_ref, gc_ref, bdk_ref, bf_ref,
                   qlat_ref, ckv_ref, qi_ref, ki_ref, wi_ref, lf_ref, qf_ref, kf_ref, vf_ref,
                   *, w_dsa, w_idx, n_idx_heads, w_fox):
    xn = _rms(x_ref[...], g_ref[...]).astype(BF16)
    pos = [0]

    def proj(width):
        c0 = pos[0]
        pos[0] = c0 + width
        return jnp.dot(xn, w_ref[:, c0:c0 + width], preferred_element_type=F32)

    qk_scale = HEAD_DIM ** -0.5 * LOG2E
    idx_scale = D_IDX ** -0.5
    qa = proj(w_dsa).astype(BF16)
    qlat = jnp.dot(qa, bdk_ref[...], preferred_element_type=F32)
    for h in range(qlat_ref.shape[0]):
        qlat_ref[h] = (qlat[:, h * LANES:(h + 1) * LANES] * qk_scale).astype(BF16)
    ckv_ref[...] = _rms(proj(LANES), gc_ref[...]).astype(BF16)
    qi_ref[...] = (proj(w_idx) * idx_scale).astype(BF16)
    ki_ref[...] = proj(LANES).astype(BF16)
    wi_ref[...] = proj(LANES)[:, :n_idx_heads] * (n_idx_heads ** -0.5)
    n_fox = bf_ref.shape[1]
    f = proj(LANES)[:, :n_fox] + bf_ref[...]
    lf_ref[...] = jnp.minimum(f, 0.0) - jnp.log1p(jnp.exp(-jnp.abs(f)))
    for hp in range(w_fox // LANES):
        qf_ref[hp] = (proj(LANES) * qk_scale).astype(BF16)
    for hp in range(w_fox // LANES):
        kf_ref[hp] = proj(LANES).astype(BF16)
    for hp in range(w_fox // LANES):
        vf_ref[hp] = proj(LANES).astype(BF16)


def _mix_in(h, g, w2, gc, bdk, bfg, l, *, w_dsa, w_idx, w_fox, d_lat, tm=512):
    t, d = h.shape
    tm = min(tm, t)
    n_dsa = w_dsa // HEAD_DIM
    n_fox = bfg.shape[2]
    npair = w_fox // LANES
    row = lambda i: (i, 0)
    row3 = lambda i: (0, i, 0)
    out_shape = [
        jax.ShapeDtypeStruct((n_dsa, t, d_lat), BF16),
        jax.ShapeDtypeStruct((t, d_lat), BF16),
        jax.ShapeDtypeStruct((t, w_idx), BF16),
        jax.ShapeDtypeStruct((t, LANES), BF16),
        jax.ShapeDtypeStruct((t, H_IDX), F32),
        jax.ShapeDtypeStruct((t, n_fox), F32),
        jax.ShapeDtypeStruct((npair, t, LANES), BF16),
        jax.ShapeDtypeStruct((npair, t, LANES), BF16),
        jax.ShapeDtypeStruct((npair, t, LANES), BF16),
    ]
    out_specs = [
        pl.BlockSpec((n_dsa, tm, d_lat), row3),
        pl.BlockSpec((tm, d_lat), row),
        pl.BlockSpec((tm, w_idx), row),
        pl.BlockSpec((tm, LANES), row),
        pl.BlockSpec((tm, H_IDX), row),
        pl.BlockSpec((tm, n_fox), row),
        pl.BlockSpec((npair, tm, LANES), row3),
        pl.BlockSpec((npair, tm, LANES), row3),
        pl.BlockSpec((npair, tm, LANES), row3),
    ]
    return pl.pallas_call(
        functools.partial(_mix_in_kernel, w_dsa=w_dsa, w_idx=w_idx, n_idx_heads=H_IDX, w_fox=w_fox),
        grid=(t // tm,),
        in_specs=[
            pl.BlockSpec((tm, d), row),
            _layer(g, l),
            _layer(w2, l),
            _layer(gc, l),
            _layer(bdk, l),
            _layer(bfg, l),
        ],
        out_specs=out_specs,
        out_shape=out_shape,
        compiler_params=_params(("parallel",)),
        name="mix_in",
    )(h, g, w2, gc, bdk, bfg)


def _cumsum_kernel(x_ref, o_ref):
    x = x_ref[...]
    n = x.shape[1]
    lane = lax.broadcasted_iota(I32, x.shape, 1)
    sh = 1
    while sh < n:
        x = x + jnp.where(lane >= sh, pltpu.roll(x, sh, axis=1), 0.0)
        sh *= 2
    o_ref[...] = x * LOG2E


def _cumsum_rows(x):
    r, n = x.shape
    rb = 8 if r % 8 == 0 else r
    return pl.pallas_call(
        _cumsum_kernel,
        grid=(r // rb,),
        in_specs=[pl.BlockSpec((rb, n), lambda i: (i, 0))],
        out_specs=pl.BlockSpec((rb, n), lambda i: (i, 0)),
        out_shape=jax.ShapeDtypeStruct((r, n), F32),
        compiler_params=_params(("parallel",)),
        name="cumsum",
    )(x)


def _bias_tiles_kernel(rb_ref, bucket_ref, o_ref):
    n_buckets, n_heads = rb_ref.shape
    for j in range(bucket_ref.shape[0]):
        bucket = bucket_ref[j]
        for h in range(n_heads):
            acc = jnp.zeros(bucket.shape, F32)
            for b in range(n_buckets):
                acc = jnp.where(bucket == b, rb_ref[b, h], acc)
            o_ref[j, h] = (acc - rb_ref[n_buckets - 1, h]) * LOG2E


def _t5_bucket(dist, n_buckets):
    n = jnp.maximum(dist, 0)
    max_exact = n_buckets // 2
    nf = jnp.maximum(n, 1).astype(F32)
    large = max_exact + (jnp.log(nf / max_exact) / math.log(MAX_DISTANCE / max_exact)
                         * (n_buckets - max_exact)).astype(I32)
    large = jnp.minimum(large, n_buckets - 1)
    return jnp.where(n < max_exact, n, large)


def _bias_tiles(rel_bias, tile):
    n_buckets, n_heads = rel_bias.shape
    assert tile >= MAX_DISTANCE
    assert N_BIAS_TILES == FAR_TILES + 1
    r = lax.broadcasted_iota(I32, (tile, tile), 0)
    c = lax.broadcasted_iota(I32, (tile, tile), 1)
    buckets = jnp.stack([_t5_bucket(j * tile + r - c, n_buckets) for j in range(N_BIAS_TILES)])
    return pl.pallas_call(
        _bias_tiles_kernel,
        in_specs=[pl.BlockSpec(memory_space=pltpu.SMEM), pl.BlockSpec(memory_space=pltpu.VMEM)],
        out_specs=pl.BlockSpec(memory_space=pltpu.VMEM),
        out_shape=jax.ShapeDtypeStruct((N_BIAS_TILES, n_heads, tile, tile), F32),
        compiler_params=pltpu.CompilerParams(vmem_limit_bytes=V7X_VMEM_LIMIT),
        name="bias_tiles",
    )(rel_bias, buckets)


def _softmax_strip(s, rows, p_ref, alpha_ref, m_ref, l_ref):
    width = s.shape[1]
    reps = width // LANES
    m_old = m_ref[rows, :]
    m_new = jnp.maximum(m_old, jnp.max(s, axis=1, keepdims=True))
    alpha = jnp.exp2(m_old - m_new)
    p = jnp.exp2(s - jnp.tile(m_new, (1, reps)))
    psum = p[:, 0:LANES]
    for j in range(1, reps):
        psum = psum + p[:, j * LANES:(j + 1) * LANES]
    l_ref[rows, :] = alpha * l_ref[rows, :] + psum
    alpha_ref[rows, :] = alpha
    m_ref[rows, :] = m_new
    p_ref[rows, 0:width] = p.astype(BF16)


def _dsa_kernel(qlat_ref, qi_ref, wi_ref, ki_ref, ckv_ref, bias_ref, o_ref,
                keys_ref, keyst_ref, thr_ref, last_ref, qidx_ref, widx_ref,
                s_ref, p_ref, alpha_ref, m_ref, l_ref, acc_ref,
                *, tile, k_sel, seq, strip):
    n_heads, _, d_lat = qlat_ref.shape
    i = pl.program_id(1)
    n_chunks = i + 1
    row = lax.broadcasted_iota(I32, (tile, tile), 0)
    col = lax.broadcasted_iota(I32, (tile, tile), 1)
    diff = col - row
    lane = lax.broadcasted_iota(I32, (tile, LANES), 1)
    low_half = lane < D_IDX

    def chunk_start(kc):
        return pl.multiple_of(kc * tile, tile)

    qi = qi_ref[...]
    wi = wi_ref[...]
    zero = jnp.zeros((), BF16)
    for h in range(H_IDX):
        grp = qi[:, (h // 2) * LANES:(h // 2 + 1) * LANES]
        qidx_ref[h * tile:(h + 1) * tile, :] = (jnp.where(low_half, grp, zero) if h % 2 == 0
                                                 else jnp.where(low_half, zero, grp))
        widx_ref[h] = jnp.broadcast_to(wi[:, h:h + 1], (tile, LANES))

    def score_chunk(start, width, j):
        k2 = ki_ref[pl.ds(start, width), :]
        acc = None
        for h in range(H_IDX):
            r = lax.dot_general(qidx_ref[h * tile:(h + 1) * tile, :], k2, _NT, preferred_element_type=F32)
            term = jnp.maximum(r, 0.0) * jnp.tile(widx_ref[h], (1, width // LANES))
            acc = term if acc is None else acc + term
        if j is not None:
            acc = jnp.where(diff <= j * tile, acc, NEG)
        bits = pltpu.bitcast(acc, I32)
        key = jnp.where(bits < 0, INT_MIN - bits, bits)
        keys_ref[:, pl.ds(start, width)] = key
        keyst_ref[pl.ds(start, width), :] = key.T

    score_width = FAR_TILES * tile

    def score_far(g, carry):
        score_chunk(pl.multiple_of(g * score_width, score_width), score_width, None)
        return carry

    def score_near(kc, carry):
        score_chunk(chunk_start(kc), tile, i - kc)
        return carry

    n_score_groups = i // FAR_TILES
    lax.fori_loop(0, n_score_groups, score_far, 0)
    lax.fori_loop(n_score_groups * FAR_TILES, i + 1, score_near, 0)

    groups = tile // SUBLANES

    def key_tile(c):
        return keyst_ref[pl.ds(chunk_start(c), tile), :].reshape(groups, SUBLANES, tile)

    def count(pred):
        def one(c, cnt):
            return cnt + jnp.sum(pred(key_tile(c), c).astype(I32), axis=0)

        def two(g, cnt):
            return one(2 * g + 1, one(2 * g, cnt))

        cnt = lax.fori_loop(0, n_chunks // 2, two, jnp.zeros((SUBLANES, tile), I32))
        cnt = lax.fori_loop(2 * (n_chunks // 2), n_chunks, one, cnt)
        return jnp.broadcast_to(jnp.sum(cnt, axis=0, keepdims=True), (SUBLANES, tile))

    def to_rows(x):
        return jnp.broadcast_to(x[0:1, :], (LANES, tile)).T

    n_nonneg = count(lambda k, c: k >= 0)
    start_high = n_nonneg >= k_sel
    prefix = jnp.where(start_high, 0, INT_MIN).astype(I32)
    n_ge = jnp.where(start_high, n_nonneg, n_chunks * tile)

    def bit_body(it, state):
        prefix, n_ge = state
        cand = prefix | lax.shift_left(jnp.int32(1), 30 - it)
        cnt = count(lambda k, c: k >= cand[None])
        keep = cnt >= k_sel
        return jnp.where(keep, cand, prefix), jnp.where(keep, cnt, n_ge)

    thr, n_ge = lax.fori_loop(0, 31, bit_body, (prefix, n_ge))
    thr_ref[...] = to_rows(thr)


    @pl.when(jnp.max(n_ge) > k_sel)
    def _():
        need = k_sel - count(lambda k, c: k > thr[None])
        pos_bits = max(seq - 1, 1).bit_length()
        pos = row.reshape(groups, SUBLANES, tile)

        def pos_body(it, x):
            cand = x | lax.shift_left(jnp.int32(1), pos_bits - 1 - it)
            f = count(lambda k, c: (k == thr[None]) & (pos + c * tile < cand[None]))
            return jnp.where(f < need, cand, x)

        last = lax.fori_loop(0, pos_bits, pos_body, jnp.zeros((SUBLANES, tile), I32))
        last_ref[...] = to_rows(last)
        reps = tile // LANES

        def drop_body(c, carry):
            cols = pl.ds(chunk_start(c), tile)
            k = keys_ref[:, cols]
            drop = (k == jnp.tile(thr_ref[...], (1, reps))) & (col + c * tile > jnp.tile(last_ref[...], (1, reps)))
            keys_ref[:, cols] = jnp.where(drop, INT_MIN, k)
            return carry

        lax.fori_loop(0, n_chunks, drop_body, 0)

    m_ref[...] = jnp.full(m_ref.shape, NEG, F32)
    l_ref[...] = jnp.zeros(l_ref.shape, F32)
    acc_ref[...] = jnp.zeros(acc_ref.shape, F32)
    q_all = qlat_ref[...].reshape(n_heads * tile, d_lat)
    sdiff = diff[0:strip, :]

    def attn_chunk(start, width, j):
        cols = pl.ds(start, width)
        c = ckv_ref[cols, :]
        s_ref[:, 0:width] = lax.dot_general(q_all, c, _NT, preferred_element_type=F32)
        for r0 in range(0, tile, strip):
            qrows = pl.ds(r0, strip)
            sel = keys_ref[qrows, cols] >= jnp.tile(thr_ref[qrows, :], (1, width // LANES))
            if j is not None:
                sel = sel & (sdiff - r0 <= j * tile)
            penalty = jnp.where(sel, 0.0, NEG)
            for h in range(n_heads):
                rows = pl.ds(h * tile + r0, strip)
                s = s_ref[rows, 0:width] + penalty
                if j is not None:
                    s = s + bias_ref[j, h, qrows, :]
                _softmax_strip(s, rows, p_ref, alpha_ref, m_ref, l_ref)
        acc_ref[...] = alpha_ref[...] * acc_ref[...] + jnp.dot(p_ref[:, 0:width], c,
                                                                preferred_element_type=F32)

    far_width = FAR_TILES * tile

    def far_body(g, carry):
        attn_chunk(pl.multiple_of(g * far_width, far_width), far_width, None)
        return carry

    def near_body(kc, carry):
        attn_chunk(chunk_start(kc), tile, i - kc)
        return carry

    n_far_groups = jnp.maximum(i - 1, 0) // FAR_TILES
    lax.fori_loop(0, n_far_groups, far_body, 0)
    lax.fori_loop(n_far_groups * FAR_TILES, i + 1, near_body, 0)

    for h in range(n_heads):
        rows = slice(h * tile, (h + 1) * tile)
        l = jnp.sum(l_ref[rows, :], axis=1, keepdims=True)
        o_ref[:, h * d_lat:(h + 1) * d_lat] = (acc_ref[rows, :] / l).astype(BF16)


def _dsa(qlat, qi, wi, ki2, ckv, bias_tiles, *, k_sel, tile):
    n_heads, b, s, d_lat = qlat.shape
    assert d_lat == LANES
    w_idx = qi.shape[2]
    qblk = lambda bi, i: (bi, i, 0)
    whole = lambda bi, i: (bi, 0, 0)
    return pl.pallas_call(
        functools.partial(_dsa_kernel, tile=tile, k_sel=k_sel, seq=s, strip=min(ATTN_STRIP, tile)),
        grid=(b, s // tile),
        in_specs=[
            pl.BlockSpec((n_heads, None, tile, d_lat), lambda bi, i: (0, bi, i, 0)),
            pl.BlockSpec((None, tile, w_idx), qblk),
            pl.BlockSpec((None, tile, H_IDX), qblk),
            pl.BlockSpec((None, s, LANES), whole),
            pl.BlockSpec((None, s, d_lat), whole),
            _resident(bias_tiles.shape),
        ],
        out_specs=pl.BlockSpec((None, tile, n_heads * d_lat), qblk),
        out_shape=jax.ShapeDtypeStruct((b, s, n_heads * d_lat), BF16),
        scratch_shapes=[
            pltpu.VMEM((tile, s), I32),
            pltpu.VMEM((s, tile), I32),
            pltpu.VMEM((tile, LANES), I32),
            pltpu.VMEM((tile, LANES), I32),
            pltpu.VMEM((H_IDX * tile, LANES), BF16),
            pltpu.VMEM((H_IDX, tile, LANES), F32),
            pltpu.VMEM((n_heads * tile, FAR_TILES * tile), F32),
            pltpu.VMEM((n_heads * tile, FAR_TILES * tile), BF16),
            pltpu.VMEM((n_heads * tile, LANES), F32),
            pltpu.VMEM((n_heads * tile, LANES), F32),
            pltpu.VMEM((n_heads * tile, LANES), F32),
            pltpu.VMEM((n_heads * tile, d_lat), F32),
        ],
        compiler_params=_params(("parallel", "arbitrary")),
        name="dsa",
    )(qlat, qi, wi, ki2, ckv, bias_tiles)


def _fox_kernel(q_ref, k_ref, v_ref, cq_ref, ck_ref, o_ref,
                s_ref, p_ref, cqr_ref, alpha_ref, m_ref, l_ref, acc_ref, *, tile, strip):
    n_pairs = q_ref.shape[0]
    i = pl.program_id(1)
    sdiff = (lax.broadcasted_iota(I32, (strip, tile), 1)
             - lax.broadcasted_iota(I32, (strip, tile), 0))
    lane = lax.broadcasted_iota(I32, (tile, LANES), 1)
    low_half = lane < HEAD_DIM
    zero = jnp.zeros((), BF16)
    q_all = []
    for pr in range(n_pairs):
        q = q_ref[pr]
        q_all.append(jnp.concatenate([jnp.where(low_half, q, zero), jnp.where(low_half, zero, q)], axis=0))
        for h in range(2):
            cqr_ref[(2 * pr + h) * tile:(2 * pr + h + 1) * tile, :] = jnp.broadcast_to(cq_ref[pr, h], (tile, LANES))
    m_ref[...] = jnp.full(m_ref.shape, NEG, F32)
    l_ref[...] = jnp.zeros(l_ref.shape, F32)
    acc_ref[...] = jnp.zeros(acc_ref.shape, F32)

    def chunk(start, width, j):
        cols = pl.ds(start, width)
        for pr in range(n_pairs):
            s_ref[pr * 2 * tile:(pr + 1) * 2 * tile, 0:width] = lax.dot_general(
                q_all[pr], k_ref[pr, cols, :], _NT, preferred_element_type=F32)
        ck = [ck_ref[pr, :, cols] for pr in range(n_pairs)]
        for r0 in range(0, tile, strip):
            if j is not None:
                penalty = jnp.where(sdiff - r0 <= j * tile, 0.0, NEG)
            for pr in range(n_pairs):
                for h in range(2):
                    rows = pl.ds((2 * pr + h) * tile + r0, strip)
                    s = s_ref[rows, 0:width] + jnp.tile(cqr_ref[rows, :], (1, width // LANES)) - ck[pr][h:h + 1, :]
                    if j is not None:
                        s = s + penalty
                    _softmax_strip(s, rows, p_ref, alpha_ref, m_ref, l_ref)
        for pr in range(n_pairs):
            rows = slice(pr * 2 * tile, (pr + 1) * 2 * tile)
            acc_ref[rows, :] = alpha_ref[rows, :] * acc_ref[rows, :] + jnp.dot(
                p_ref[rows, 0:width], v_ref[pr, cols, :], preferred_element_type=F32)

    far_width = FAR_TILES * tile

    def far_body(g, carry):
        chunk(pl.multiple_of(g * far_width, far_width), far_width, None)
        return carry

    def near_body(kc, carry):
        chunk(pl.multiple_of(kc * tile, tile), tile, i - kc)
        return carry

    n_far_groups = i // FAR_TILES
    lax.fori_loop(0, n_far_groups, far_body, 0)
    lax.fori_loop(n_far_groups * FAR_TILES, i + 1, near_body, 0)
    for pr in range(n_pairs):
        r0 = 2 * pr * tile
        o0 = acc_ref[r0:r0 + tile, :] / jnp.sum(l_ref[r0:r0 + tile, :], axis=1, keepdims=True)
        o1 = acc_ref[r0 + tile:r0 + 2 * tile, :] / jnp.sum(l_ref[r0 + tile:r0 + 2 * tile, :], axis=1, keepdims=True)
        o_ref[pr] = jnp.where(low_half, o0, o1).astype(BF16)


def _fox(qf, kf, vf, cum_col, cum_row, *, tile):
    npair, b, s, _ = qf.shape
    rows = 2 * npair * tile
    qblk = lambda bi, i: (0, bi, i, 0)
    whole = lambda bi, i: (0, bi, 0, 0)
    return pl.pallas_call(
        functools.partial(_fox_kernel, tile=tile, strip=min(ATTN_STRIP, tile)),
        grid=(b, s // tile),
        in_specs=[
            pl.BlockSpec((npair, None, tile, LANES), qblk),
            pl.BlockSpec((npair, None, s, LANES), whole),
            pl.BlockSpec((npair, None, s, LANES), whole),
            pl.BlockSpec((None, npair, 2, tile, 1), lambda bi, i: (bi, 0, 0, i, 0)),
            pl.BlockSpec((None, npair, 2, s), lambda bi, i: (bi, 0, 0, 0)),
        ],
        out_specs=pl.BlockSpec((npair, None, tile, LANES), qblk),
        out_shape=jax.ShapeDtypeStruct((npair, b, s, LANES), BF16),
        scratch_shapes=[
            pltpu.VMEM((rows, FAR_TILES * tile), F32),
            pltpu.VMEM((rows, FAR_TILES * tile), BF16),
            pltpu.VMEM((rows, LANES), F32),
            pltpu.VMEM((rows, LANES), F32),
            pltpu.VMEM((rows, LANES), F32),
            pltpu.VMEM((rows, LANES), F32),
            pltpu.VMEM((rows, LANES), F32),
        ],
        compiler_params=_params(("parallel", "arbitrary")),
        name="fox",
    )(qf, kf, vf, cum_col, cum_row)


def _mix_out_ffn_kernel(h_ref, ol_ref, of_ref, bdv_ref, wo_ref, g_ref, wg_ref, wu_ref, wd_ref, fg_ref,
                        o_ref, acc_ref, *, ff_chunk, final):
    oa = jnp.dot(ol_ref[...], bdv_ref[...], preferred_element_type=F32).astype(BF16)
    w_dsa = oa.shape[1]
    y = jnp.dot(oa, wo_ref[0:w_dsa, :], preferred_element_type=F32)
    for hp in range(of_ref.shape[0]):
        r0 = w_dsa + hp * LANES
        y = y + jnp.dot(of_ref[hp], wo_ref[r0:r0 + LANES, :], preferred_element_type=F32)
    out = _swiglu_half_step(h_ref[...] + y, g_ref, wg_ref, wu_ref, wd_ref, acc_ref, ff_chunk)
    if final:
        out = _rms(out, fg_ref[...])
    o_ref[...] = out


def _mix_out_ffn(h, olat, of, bdv, wo, g, wg, wu, wd, fg, l, *, final, tm=512, ff_chunk=256):
    t, d = h.shape
    tm = min(tm, t)
    npair = of.shape[0]
    row = lambda i: (i, 0)
    return pl.pallas_call(
        functools.partial(_mix_out_ffn_kernel, ff_chunk=ff_chunk, final=final),
        grid=(t // tm,),
        in_specs=[
            pl.BlockSpec((tm, d), row),
            pl.BlockSpec((tm, olat.shape[1]), row),
            pl.BlockSpec((npair, tm, LANES), lambda i: (0, i, 0)),
            _layer(bdv, l),
            _layer(wo, l),
            _layer(g, l),
            _layer(wg, l),
            _layer(wu, l),
            _layer(wd, l),
            _resident((1, d)),
        ],
        out_specs=pl.BlockSpec((tm, d), row),
        out_shape=jax.ShapeDtypeStruct((t, d), F32),
        scratch_shapes=[pltpu.VMEM((tm, d), F32)],
        compiler_params=_params(("parallel",)),
        name="mix_out_ffn",
    )(h, olat, of, bdv, wo, g, wg, wu, wd, fg)


def _cast_kernel(x_ref, o_ref):
    o_ref[...] = x_ref[...].astype(o_ref.dtype)


def _to_bf16(x, *, rows=512):
    x2 = x.reshape(-1, x.shape[-1])
    r, c = x2.shape
    rows = min(rows, r)
    assert r % rows == 0
    out = pl.pallas_call(
        _cast_kernel,
        grid=(r // rows,),
        in_specs=[pl.BlockSpec((rows, c), lambda i: (i, 0))],
        out_specs=pl.BlockSpec((rows, c), lambda i: (i, 0)),
        out_shape=jax.ShapeDtypeStruct((r, c), BF16),
        compiler_params=_params(("parallel",)),
        name="to_bf16",
    )(x2)
    return out.reshape(x.shape)


def _pad_cols(w, width):
    return jnp.pad(w, ((0, 0), (0, 0), (0, width - w.shape[2])))


def _mixer_weights(w_in, w_uk, w_uv, n_fox):
    _, d_lat, n_dsa, hd = w_uk.shape
    w_dsa = n_dsa * hd
    w_idx = H_IDX * D_IDX
    w_fox = n_fox * hd
    sizes = [w_dsa, d_lat, w_idx, D_IDX, H_IDX, 3 * w_fox, n_fox]
    assert sum(sizes) == w_in.shape[2] and d_lat == LANES and 2 * D_IDX == LANES
    offs = [0]
    for sz in sizes:
        offs.append(offs[-1] + sz)
    q_a, c_raw, q_i, k_i, w_i, qkv_f, f_l = [w_in[:, :, offs[j]:offs[j + 1]] for j in range(7)]
    w2 = jnp.concatenate(
        [q_a, c_raw, q_i, k_i, k_i, _pad_cols(w_i, LANES), _pad_cols(f_l, LANES), qkv_f], axis=2)
    depth = w_in.shape[0]
    eye = jnp.eye(n_dsa, dtype=w_uk.dtype)
    bdk = jnp.einsum('lchd,hg->lhdgc', w_uk, eye).reshape(depth, w_dsa, n_dsa * d_lat).astype(BF16)
    bdv = jnp.einsum('lchd,hg->lhcgd', w_uv, eye).reshape(depth, n_dsa * d_lat, w_dsa).astype(BF16)
    return w2, bdk, bdv, dict(w_dsa=w_dsa, w_idx=w_idx, w_fox=w_fox, d_lat=d_lat)


def kernel(x, ffn1_norm, ffn1_gate, ffn1_up, ffn1_down, mix_norm, w_in, b_forget, ckv_norm, w_uk, w_uv, w_out,
           ffn2_norm, ffn2_gate, ffn2_up, ffn2_down, rel_bias, final_norm):
    b, s, d = x.shape
    depth = w_in.shape[0]
    n_fox = b_forget.shape[1]
    t = b * s
    tile = min(DSA_TILE, s)
    fox_tile = min(FOX_TILE, s)
    k_sel = min(TOPK_MAX, s // 4)
    npair = n_fox * HEAD_DIM // LANES

    bias_tiles = _bias_tiles(rel_bias, tile)
    fg = final_norm.reshape(1, d)

    g1, u1, d1 = _to_bf16(ffn1_gate), _to_bf16(ffn1_up), _to_bf16(ffn1_down)
    g2, u2, d2 = _to_bf16(ffn2_gate), _to_bf16(ffn2_up), _to_bf16(ffn2_down)
    wo = _to_bf16(w_out)
    w2, bdk, bdv, dims = _mixer_weights(_to_bf16(w_in), w_uk, w_uv, n_fox)
    n1, nm, n2 = (v[:, None, :] for v in (ffn1_norm, mix_norm, ffn2_norm))
    gc, bfg = ckv_norm[:, None, :], b_forget[:, None, :]

    h = x.reshape(t, d)
    for l in range(depth):
        h = _ffn(h, n1, g1, u1, d1, l)
        qlat, ckv, qi, ki2, wi, logf, qf, kf, vf = _mix_in(h, nm, w2, gc, bdk, bfg, l, **dims)

        cum_row = _cumsum_rows(jnp.swapaxes(logf.reshape(b, s, n_fox), 1, 2).reshape(b * n_fox, s))
        cum_row = cum_row.reshape(b, npair, 2, s)
        cum_col = cum_row[..., None]

        olat = _dsa(qlat.reshape(-1, b, s, LANES), qi.reshape(b, s, -1), wi.reshape(b, s, -1),
                    ki2.reshape(b, s, -1), ckv.reshape(b, s, -1), bias_tiles, k_sel=k_sel, tile=tile)
        of = _fox(qf.reshape(npair, b, s, LANES), kf.reshape(npair, b, s, LANES), vf.reshape(npair, b, s, LANES),
                  cum_col, cum_row, tile=fox_tile)

        h = _mix_out_ffn(h, olat.reshape(t, -1), of.reshape(npair, t, LANES), bdv, wo, n2, g2, u2, d2, fg, l,
                         final=(l == depth - 1))
    return h.reshape(b, s, d)
```

```python
import functools
import math

import jax
import jax.numpy as jnp
from jax import lax
from jax.experimental import pallas as pl
from jax.experimental.pallas import tpu as pltpu

HEAD_DIM = 64
H_IDX = 8
D_IDX = 64
TOPK_MAX = 256
MAX_DISTANCE = 128
EPS = 1e-6
NEG = -1e30

LANES = 128
SUBLANES = 8
V7X_VMEM_LIMIT = 56 * 1024 * 1024

DSA_TILE = 256
FOX_TILE = 256
ATTN_STRIP = 32
FAR_TILES = 2
N_BIAS_TILES = 3

LOG2E = math.log2(math.e)
INT_MIN = -(2 ** 31)
F32 = jnp.float32
BF16 = jnp.bfloat16
I32 = jnp.int32

_NT = (((1,), (1,)), ((), ()))


def _params(sem):
    return pltpu.CompilerParams(dimension_semantics=sem, vmem_limit_bytes=V7X_VMEM_LIMIT)


def _resident(shape):
    zeros = (0,) * len(shape)
    return pl.BlockSpec(shape, lambda *_: zeros, pipeline_mode=pl.Buffered(1))


def _layer(stacked, l):
    _, r, c = stacked.shape
    return pl.BlockSpec((None, r, c), lambda *_: (l, 0, 0), pipeline_mode=pl.Buffered(1))


def _rms(x, g):
    y = x * lax.rsqrt(jnp.mean(x * x, axis=-1, keepdims=True) + EPS)
    return y * g


def _swiglu_half_step(x, g_ref, wg_ref, wu_ref, wd_ref, acc_ref, ff_chunk):
    xn = _rms(x, g_ref[...]).astype(BF16)
    d_ff = wg_ref.shape[1]
    for c0 in range(0, d_ff, ff_chunk):
        c1 = min(c0 + ff_chunk, d_ff)
        g = jnp.dot(xn, wg_ref[:, c0:c1], preferred_element_type=F32)
        u = jnp.dot(xn, wu_ref[:, c0:c1], preferred_element_type=F32)
        a = (g * jax.nn.sigmoid(g) * u).astype(BF16)
        d = jnp.dot(a, wd_ref[c0:c1, :], preferred_element_type=F32)
        if c0 == 0:
            acc_ref[...] = d
        else:
            acc_ref[...] += d
    return x + 0.5 * acc_ref[...]


def _ffn_kernel(x_ref, g_ref, wg_ref, wu_ref, wd_ref, o_ref, acc_ref, *, ff_chunk):
    o_ref[...] = _swiglu_half_step(x_ref[...], g_ref, wg_ref, wu_ref, wd_ref, acc_ref, ff_chunk)


def _ffn(h, g, wg, wu, wd, l, *, tm=512, ff_chunk=256):
    t, d = h.shape
    tm = min(tm, t)
    return pl.pallas_call(
        functools.partial(_ffn_kernel, ff_chunk=ff_chunk),
        grid=(t // tm,),
        in_specs=[
            pl.BlockSpec((tm, d), lambda i: (i, 0)),
            _layer(g, l),
            _layer(wg, l),
            _layer(wu, l),
            _layer(wd, l),
        ],
        out_specs=pl.BlockSpec((tm, d), lambda i: (i, 0)),
        out_shape=jax.ShapeDtypeStruct((t, d), F32),
        scratch_shapes=[pltpu.VMEM((tm, d), F32)],
        compiler_params=_params(("parallel",)),
        name="ffn",
    )(h, g, wg, wu, wd)


def _mix_in_kernel(x_ref, g_ref, w_ref, gc_ref, bdk_ref, bf_ref,
                   qlat_ref, ckv_ref, qi_ref, ki_ref, wi_ref, lf_ref, qf_ref, kf_ref, vf_ref,
                   *, w_dsa, w_idx, n_idx_heads, w_fox):
    xn = _rms(x_ref[...], g_ref[...]).astype(BF16)
    pos = [0]

    def proj(width):
        c0 = pos[0]
        pos[0] = c0 + width
        return jnp.dot(xn, w_ref[:, c0:c0 + width], preferred_element_type=F32)

    qk_scale = HEAD_DIM ** -0.5 * LOG2E
    idx_scale = D_IDX ** -0.5
    qa = proj(w_dsa).astype(BF16)
    qlat = jnp.dot(qa, bdk_ref[...], preferred_element_type=F32)
    for h in range(qlat_ref.shape[0]):
        qlat_ref[h] = (qlat[:, h * LANES:(h + 1) * LANES] * qk_scale).astype(BF16)
    ckv_ref[...] = _rms(proj(LANES), gc_ref[...]).astype(BF16)
    qi_ref[...] = (proj(w_idx) * idx_scale).astype(BF16)
    ki_ref[...] = proj(LANES).astype(BF16)
    wi_ref[...] = proj(LANES)[:, :n_idx_heads] * (n_idx_heads ** -0.5)
    n_fox = bf_ref.shape[1]
    f = proj(LANES)[:, :n_fox] + bf_ref[...]
    lf_ref[...] = jnp.minimum(f, 0.0) - jnp.log1p(jnp.exp(-jnp.abs(f)))
    for hp in range(w_fox // LANES):
        qf_ref[hp] = (proj(LANES) * qk_scale).astype(BF16)
    for hp in range(w_fox // LANES):
        kf_ref[hp] = proj(LANES).astype(BF16)
    for hp in range(w_fox // LANES):
        vf_ref[hp] = proj(LANES).astype(BF16)


def _mix_in(h, g, w2, gc, bdk, bfg, l, *, w_dsa, w_idx, w_fox, d_lat, tm=512):
    t, d = h.shape
    tm = min(tm, t)
    n_dsa = w_dsa // HEAD_DIM
    n_fox = bfg.shape[2]
    npair = w_fox // LANES
    row = lambda i: (i, 0)
    row3 = lambda i: (0, i, 0)
    out_shape = [
        jax.ShapeDtypeStruct((n_dsa, t, d_lat), BF16),
        jax.ShapeDtypeStruct((t, d_lat), BF16),
        jax.ShapeDtypeStruct((t, w_idx), BF16),
        jax.ShapeDtypeStruct((t, LANES), BF16),
        jax.ShapeDtypeStruct((t, H_IDX), F32),
        jax.ShapeDtypeStruct((t, n_fox), F32),
        jax.ShapeDtypeStruct((npair, t, LANES), BF16),
        jax.ShapeDtypeStruct((npair, t, LANES), BF16),
        jax.ShapeDtypeStruct((npair, t, LANES), BF16),
    ]
    out_specs = [
        pl.BlockSpec((n_dsa, tm, d_lat), row3),
        pl.BlockSpec((tm, d_lat), row),
        pl.BlockSpec((tm, w_idx), row),
        pl.BlockSpec((tm, LANES), row),
        pl.BlockSpec((tm, H_IDX), row),
        pl.BlockSpec((tm, n_fox), row),
        pl.BlockSpec((npair, tm, LANES), row3),
        pl.BlockSpec((npair, tm, LANES), row3),
        pl.BlockSpec((npair, tm, LANES), row3),
    ]
    return pl.pallas_call(
        functools.partial(_mix_in_kernel, w_dsa=w_dsa, w_idx=w_idx, n_idx_heads=H_IDX, w_fox=w_fox),
        grid=(t // tm,),
        in_specs=[
            pl.BlockSpec((tm, d), row),
            _layer(g, l),
            _layer(w2, l),
            _layer(gc, l),
            _layer(bdk, l),
            _layer(bfg, l),
        ],
        out_specs=out_specs,
        out_shape=out_shape,
        compiler_params=_params(("parallel",)),
        name="mix_in",
    )(h, g, w2, gc, bdk, bfg)


def _cumsum_kernel(x_ref, o_ref):
    x = x_ref[...]
    n = x.shape[1]
    lane = lax.broadcasted_iota(I32, x.shape, 1)
    sh = 1
    while sh < n:
        x = x + jnp.where(lane >= sh, pltpu.roll(x, sh, axis=1), 0.0)
        sh *= 2
    o_ref[...] = x * LOG2E


def _cumsum_rows(x):
    r, n = x.shape
    rb = 8 if r % 8 == 0 else r
    return pl.pallas_call(
        _cumsum_kernel,
        grid=(r // rb,),
        in_specs=[pl.BlockSpec((rb, n), lambda i: (i, 0))],
        out_specs=pl.BlockSpec((rb, n), lambda i: (i, 0)),
        out_shape=jax.ShapeDtypeStruct((r, n), F32),
        compiler_params=_params(("parallel",)),
        name="cumsum",
    )(x)


def _bias_tiles_kernel(rb_ref, bucket_ref, o_ref):
    n_buckets, n_heads = rb_ref.shape
    for j in range(bucket_ref.shape[0]):
        bucket = bucket_ref[j]
        for h in range(n_heads):
            acc = jnp.zeros(bucket.shape, F32)
            for b in range(n_buckets):
                acc = jnp.where(bucket == b, rb_ref[b, h], acc)
            o_ref[j, h] = (acc - rb_ref[n_buckets - 1, h]) * LOG2E


def _t5_bucket(dist, n_buckets):
    n = jnp.maximum(dist, 0)
    max_exact = n_buckets // 2
    nf = jnp.maximum(n, 1).astype(F32)
    large = max_exact + (jnp.log(nf / max_exact) / math.log(MAX_DISTANCE / max_exact)
                         * (n_buckets - max_exact)).astype(I32)
    large = jnp.minimum(large, n_buckets - 1)
    return jnp.where(n < max_exact, n, large)


def _bias_tiles(rel_bias, tile):
    n_buckets, n_heads = rel_bias.shape
    assert tile >= MAX_DISTANCE
    assert N_BIAS_TILES == FAR_TILES + 1
    r = lax.broadcasted_iota(I32, (tile, tile), 0)
    c = lax.broadcasted_iota(I32, (tile, tile), 1)
    buckets = jnp.stack([_t5_bucket(j * tile + r - c, n_buckets) for j in range(N_BIAS_TILES)])
    return pl.pallas_call(
        _bias_tiles_kernel,
        in_specs=[pl.BlockSpec(memory_space=pltpu.SMEM), pl.BlockSpec(memory_space=pltpu.VMEM)],
        out_specs=pl.BlockSpec(memory_space=pltpu.VMEM),
        out_shape=jax.ShapeDtypeStruct((N_BIAS_TILES, n_heads, tile, tile), F32),
        compiler_params=pltpu.CompilerParams(vmem_limit_bytes=V7X_VMEM_LIMIT),
        name="bias_tiles",
    )(rel_bias, buckets)


def _softmax_strip(s, rows, p_ref, alpha_ref, m_ref, l_ref):
    width = s.shape[1]
    reps = width // LANES
    m_old = m_ref[rows, :]
    m_new = jnp.maximum(m_old, jnp.max(s, axis=1, keepdims=True))
    alpha = jnp.exp2(m_old - m_new)
    p = jnp.exp2(s - jnp.tile(m_new, (1, reps)))
    psum = p[:, 0:LANES]
    for j in range(1, reps):
        psum = psum + p[:, j * LANES:(j + 1) * LANES]
    l_ref[rows, :] = alpha * l_ref[rows, :] + psum
    alpha_ref[rows, :] = alpha
    m_ref[rows, :] = m_new
    p_ref[rows, 0:width] = p.astype(BF16)


def _dsa_kernel(qlat_ref, qi_ref, wi_ref, ki_ref, ckv_ref, bias_ref, o_ref,
                keys_ref, keyst_ref, thr_ref, last_ref, qidx_ref, widx_ref,
                s_ref, p_ref, alpha_ref, m_ref, l_ref, acc_ref,
                *, tile, k_sel, seq, strip):
    n_heads, _, d_lat = qlat_ref.shape
    i = pl.program_id(1)
    n_chunks = i + 1
    row = lax.broadcasted_iota(I32, (tile, tile), 0)
    col = lax.broadcasted_iota(I32, (tile, tile), 1)
    diff = col - row
    lane = lax.broadcasted_iota(I32, (tile, LANES), 1)
    low_half = lane < D_IDX

    def chunk_start(kc):
        return pl.multiple_of(kc * tile, tile)

    qi = qi_ref[...]
    wi = wi_ref[...]
    zero = jnp.zeros((), BF16)
    for h in range(H_IDX):
        grp = qi[:, (h // 2) * LANES:(h // 2 + 1) * LANES]
        qidx_ref[h * tile:(h + 1) * tile, :] = (jnp.where(low_half, grp, zero) if h % 2 == 0
                                                 else jnp.where(low_half, zero, grp))
        widx_ref[h] = jnp.broadcast_to(wi[:, h:h + 1], (tile, LANES))

    def score_chunk(start, width, j):
        k2 = ki_ref[pl.ds(start, width), :]
        acc = None
        for h in range(H_IDX):
            r = lax.dot_general(qidx_ref[h * tile:(h + 1) * tile, :], k2, _NT, preferred_element_type=F32)
            term = jnp.maximum(r, 0.0) * jnp.tile(widx_ref[h], (1, width // LANES))
            acc = term if acc is None else acc + term
        if j is not None:
            acc = jnp.where(diff <= j * tile, acc, NEG)
        bits = pltpu.bitcast(acc, I32)
        key = jnp.where(bits < 0, INT_MIN - bits, bits)
        keys_ref[:, pl.ds(start, width)] = key
        keyst_ref[pl.ds(start, width), :] = key.T

    score_width = FAR_TILES * tile

    def score_far(g, carry):
        score_chunk(pl.multiple_of(g * score_width, score_width), score_width, None)
        return carry

    def score_near(kc, carry):
        score_chunk(chunk_start(kc), tile, i - kc)
        return carry

    n_score_groups = i // FAR_TILES
    lax.fori_loop(0, n_score_groups, score_far, 0)
    lax.fori_loop(n_score_groups * FAR_TILES, i + 1, score_near, 0)

    groups = tile // SUBLANES

    def key_tile(c):
        return keyst_ref[pl.ds(chunk_start(c), tile), :].reshape(groups, SUBLANES, tile)

    def count(pred):
        def one(c, cnt):
            return cnt + jnp.sum(pred(key_tile(c), c).astype(I32), axis=0)

        def two(g, cnt):
            return one(2 * g + 1, one(2 * g, cnt))

        cnt = lax.fori_loop(0, n_chunks // 2, two, jnp.zeros((SUBLANES, tile), I32))
        cnt = lax.fori_loop(2 * (n_chunks // 2), n_chunks, one, cnt)
        return jnp.broadcast_to(jnp.sum(cnt, axis=0, keepdims=True), (SUBLANES, tile))

    def to_rows(x):
        return jnp.broadcast_to(x[0:1, :], (LANES, tile)).T

    n_nonneg = count(lambda k, c: k >= 0)
    start_high = n_nonneg >= k_sel
    prefix = jnp.where(start_high, 0, INT_MIN).astype(I32)
    n_ge = jnp.where(start_high, n_nonneg, n_chunks * tile)

    def bit_body(it, state):
        prefix, n_ge = state
        cand = prefix | lax.shift_left(jnp.int32(1), 30 - it)
        cnt = count(lambda k, c: k >= cand[None])
        keep = cnt >= k_sel
        return jnp.where(keep, cand, prefix), jnp.where(keep, cnt, n_ge)

    thr, n_ge = lax.fori_loop(0, 31, bit_body, (prefix, n_ge))
    thr_ref[...] = to_rows(thr)


    @pl.when(jnp.max(n_ge) > k_sel)
    def _():
        need = k_sel - count(lambda k, c: k > thr[None])
        pos_bits = max(seq - 1, 1).bit_length()
        pos = row.reshape(groups, SUBLANES, tile)

        def pos_body(it, x):
            cand = x | lax.shift_left(jnp.int32(1), pos_bits - 1 - it)
            f = count(lambda k, c: (k == thr[None]) & (pos + c * tile < cand[None]))
            return jnp.where(f < need, cand, x)

        last = lax.fori_loop(0, pos_bits, pos_body, jnp.zeros((SUBLANES, tile), I32))
        last_ref[...] = to_rows(last)
        reps = tile // LANES

        def drop_body(c, carry):
            cols = pl.ds(chunk_start(c), tile)
            k = keys_ref[:, cols]
            drop = (k == jnp.tile(thr_ref[...], (1, reps))) & (col + c * tile > jnp.tile(last_ref[...], (1, reps)))
            keys_ref[:, cols] = jnp.where(drop, INT_MIN, k)
            return carry

        lax.fori_loop(0, n_chunks, drop_body, 0)

    m_ref[...] = jnp.full(m_ref.shape, NEG, F32)
    l_ref[...] = jnp.zeros(l_ref.shape, F32)
    acc_ref[...] = jnp.zeros(acc_ref.shape, F32)
    q_all = qlat_ref[...].reshape(n_heads * tile, d_lat)
    sdiff = diff[0:strip, :]

    def attn_chunk(start, width, j):
        cols = pl.ds(start, width)
        c = ckv_ref[cols, :]
        s_ref[:, 0:width] = lax.dot_general(q_all, c, _NT, preferred_element_type=F32)
        for r0 in range(0, tile, strip):
            qrows = pl.ds(r0, strip)
            sel = keys_ref[qrows, cols] >= jnp.tile(thr_ref[qrows, :], (1, width // LANES))
            if j is not None:
                sel = sel & (sdiff - r0 <= j * tile)
            penalty = jnp.where(sel, 0.0, NEG)
            for h in range(n_heads):
                rows = pl.ds(h * tile + r0, strip)
                s = s_ref[rows, 0:width] + penalty
                if j is not None:
                    s = s + bias_ref[j, h, qrows, :]
                _softmax_strip(s, rows, p_ref, alpha_ref, m_ref, l_ref)
        acc_ref[...] = alpha_ref[...] * acc_ref[...] + jnp.dot(p_ref[:, 0:width], c,
                                                                preferred_element_type=F32)

    far_width = FAR_TILES * tile

    def far_body(g, carry):
        attn_chunk(pl.multiple_of(g * far_width, far_width), far_width, None)
        return carry

    def near_body(kc, carry):
        attn_chunk(chunk_start(kc), tile, i - kc)
        return carry

    n_far_groups = jnp.maximum(i - 1, 0) // FAR_TILES
    lax.fori_loop(0, n_far_groups, far_body, 0)
    lax.fori_loop(n_far_groups * FAR_TILES, i + 1, near_body, 0)

    for h in range(n_heads):
        rows = slice(h * tile, (h + 1) * tile)
        l = jnp.sum(l_ref[rows, :], axis=1, keepdims=True)
        o_ref[:, h * d_lat:(h + 1) * d_lat] = (acc_ref[rows, :] / l).astype(BF16)


def _dsa(qlat, qi, wi, ki2, ckv, bias_tiles, *, k_sel, tile):
    n_heads, b, s, d_lat = qlat.shape
    assert d_lat == LANES
    w_idx = qi.shape[2]
    qblk = lambda bi, i: (bi, i, 0)
    whole = lambda bi, i: (bi, 0, 0)
    return pl.pallas_call(
        functools.partial(_dsa_kernel, tile=tile, k_sel=k_sel, seq=s, strip=min(ATTN_STRIP, tile)),
        grid=(b, s // tile),
        in_specs=[
            pl.BlockSpec((n_heads, None, tile, d_lat), lambda bi, i: (0, bi, i, 0)),
            pl.BlockSpec((None, tile, w_idx), qblk),
            pl.BlockSpec((None, tile, H_IDX), qblk),
            pl.BlockSpec((None, s, LANES), whole),
            pl.BlockSpec((None, s, d_lat), whole),
            _resident(bias_tiles.shape),
        ],
        out_specs=pl.BlockSpec((None, tile, n_heads * d_lat), qblk),
        out_shape=jax.ShapeDtypeStruct((b, s, n_heads * d_lat), BF16),
        scratch_shapes=[
            pltpu.VMEM((tile, s), I32),
            pltpu.VMEM((s, tile), I32),
            pltpu.VMEM((tile, LANES), I32),
            pltpu.VMEM((tile, LANES), I32),
            pltpu.VMEM((H_IDX * tile, LANES), BF16),
            pltpu.VMEM((H_IDX, tile, LANES), F32),
            pltpu.VMEM((n_heads * tile, FAR_TILES * tile), F32),
            pltpu.VMEM((n_heads * tile, FAR_TILES * tile), BF16),
            pltpu.VMEM((n_heads * tile, LANES), F32),
            pltpu.VMEM((n_heads * tile, LANES), F32),
            pltpu.VMEM((n_heads * tile, LANES), F32),
            pltpu.VMEM((n_heads * tile, d_lat), F32),
        ],
        compiler_params=_params(("parallel", "arbitrary")),
        name="dsa",
    )(qlat, qi, wi, ki2, ckv, bias_tiles)


def _fox_kernel(q_ref, k_ref, v_ref, cq_ref, ck_ref, o_ref,
                s_ref, p_ref, cqr_ref, alpha_ref, m_ref, l_ref, acc_ref, *, tile, strip):
    n_pairs = q_ref.shape[0]
    i = pl.program_id(1)
    sdiff = (lax.broadcasted_iota(I32, (strip, tile), 1)
             - lax.broadcasted_iota(I32, (strip, tile), 0))
    lane = lax.broadcasted_iota(I32, (tile, LANES), 1)
    low_half = lane < HEAD_DIM
    zero = jnp.zeros((), BF16)
    q_all = []
    for pr in range(n_pairs):
        q = q_ref[pr]
        q_all.append(jnp.concatenate([jnp.where(low_half, q, zero), jnp.where(low_half, zero, q)], axis=0))
        for h in range(2):
            cqr_ref[(2 * pr + h) * tile:(2 * pr + h + 1) * tile, :] = jnp.broadcast_to(
                cq_ref[pr, h:h + 1, :], (LANES, tile)).T
    m_ref[...] = jnp.full(m_ref.shape, NEG, F32)
    l_ref[...] = jnp.zeros(l_ref.shape, F32)
    acc_ref[...] = jnp.zeros(acc_ref.shape, F32)

    def chunk(start, width, j):
        cols = pl.ds(start, width)
        for pr in range(n_pairs):
            s_ref[pr * 2 * tile:(pr + 1) * 2 * tile, 0:width] = lax.dot_general(
                q_all[pr], k_ref[pr, cols, :], _NT, preferred_element_type=F32)
        ck = [ck_ref[pr, :, cols] for pr in range(n_pairs)]
        for r0 in range(0, tile, strip):
            if j is not None:
                penalty = jnp.where(sdiff - r0 <= j * tile, 0.0, NEG)
            for pr in range(n_pairs):
                for h in range(2):
                    rows = pl.ds((2 * pr + h) * tile + r0, strip)
                    s = s_ref[rows, 0:width] + jnp.tile(cqr_ref[rows, :], (1, width // LANES)) - ck[pr][h:h + 1, :]
                    if j is not None:
                        s = s + penalty
                    _softmax_strip(s, rows, p_ref, alpha_ref, m_ref, l_ref)
        for pr in range(n_pairs):
            rows = slice(pr * 2 * tile, (pr + 1) * 2 * tile)
            acc_ref[rows, :] = alpha_ref[rows, :] * acc_ref[rows, :] + jnp.dot(
                p_ref[rows, 0:width], v_ref[pr, cols, :], preferred_element_type=F32)

    far_width = FAR_TILES * tile

    def far_body(g, carry):
        chunk(pl.multiple_of(g * far_width, far_width), far_width, None)
        return carry

    def near_body(kc, carry):
        chunk(pl.multiple_of(kc * tile, tile), tile, i - kc)
        return carry

    n_far_groups = i // FAR_TILES
    lax.fori_loop(0, n_far_groups, far_body, 0)
    lax.fori_loop(n_far_groups * FAR_TILES, i + 1, near_body, 0)
    for pr in range(n_pairs):
        r0 = 2 * pr * tile
        o0 = acc_ref[r0:r0 + tile, :] / jnp.sum(l_ref[r0:r0 + tile, :], axis=1, keepdims=True)
        o1 = acc_ref[r0 + tile:r0 + 2 * tile, :] / jnp.sum(l_ref[r0 + tile:r0 + 2 * tile, :], axis=1, keepdims=True)
        o_ref[pr] = jnp.where(low_half, o0, o1).astype(BF16)


def _fox(qf, kf, vf, cum, *, tile):
    npair, b, s, _ = qf.shape
    rows = 2 * npair * tile
    qblk = lambda bi, i: (0, bi, i, 0)
    whole = lambda bi, i: (0, bi, 0, 0)
    return pl.pallas_call(
        functools.partial(_fox_kernel, tile=tile, strip=min(ATTN_STRIP, tile)),
        grid=(b, s // tile),
        in_specs=[
            pl.BlockSpec((npair, None, tile, LANES), qblk),
            pl.BlockSpec((npair, None, s, LANES), whole),
            pl.BlockSpec((npair, None, s, LANES), whole),
            pl.BlockSpec((None, npair, 2, tile), lambda bi, i: (bi, 0, 0, i)),
            pl.BlockSpec((None, npair, 2, s), lambda bi, i: (bi, 0, 0, 0)),
        ],
        out_specs=pl.BlockSpec((npair, None, tile, LANES), qblk),
        out_shape=jax.ShapeDtypeStruct((npair, b, s, LANES), BF16),
        scratch_shapes=[
            pltpu.VMEM((rows, FAR_TILES * tile), F32),
            pltpu.VMEM((rows, FAR_TILES * tile), BF16),
            pltpu.VMEM((rows, LANES), F32),
            pltpu.VMEM((rows, LANES), F32),
            pltpu.VMEM((rows, LANES), F32),
            pltpu.VMEM((rows, LANES), F32),
            pltpu.VMEM((rows, LANES), F32),
        ],
        compiler_params=_params(("parallel", "arbitrary")),
        name="fox",
    )(qf, kf, vf, cum, cum)


def _mix_out_ffn_kernel(h_ref, ol_ref, of_ref, bdv_ref, wo_ref, g_ref, wg_ref, wu_ref, wd_ref, fg_ref,
                        o_ref, acc_ref, *, ff_chunk, final):
    oa = jnp.dot(ol_ref[...], bdv_ref[...], preferred_element_type=F32).astype(BF16)
    w_dsa = oa.shape[1]
    y = jnp.dot(oa, wo_ref[0:w_dsa, :], preferred_element_type=F32)
    for hp in range(of_ref.shape[0]):
        r0 = w_dsa + hp * LANES
        y = y + jnp.dot(of_ref[hp], wo_ref[r0:r0 + LANES, :], preferred_element_type=F32)
    out = _swiglu_half_step(h_ref[...] + y, g_ref, wg_ref, wu_ref, wd_ref, acc_ref, ff_chunk)
    if final:
        out = _rms(out, fg_ref[...])
    o_ref[...] = out


def _mix_out_ffn(h, olat, of, bdv, wo, g, wg, wu, wd, fg, l, *, final, tm=512, ff_chunk=256):
    t, d = h.shape
    tm = min(tm, t)
    npair = of.shape[0]
    row = lambda i: (i, 0)
    return pl.pallas_call(
        functools.partial(_mix_out_ffn_kernel, ff_chunk=ff_chunk, final=final),
        grid=(t // tm,),
        in_specs=[
            pl.BlockSpec((tm, d), row),
            pl.BlockSpec((tm, olat.shape[1]), row),
            pl.BlockSpec((npair, tm, LANES), lambda i: (0, i, 0)),
            _layer(bdv, l),
            _layer(wo, l),
            _layer(g, l),
            _layer(wg, l),
            _layer(wu, l),
            _layer(wd, l),
            _resident((1, d)),
        ],
        out_specs=pl.BlockSpec((tm, d), row),
        out_shape=jax.ShapeDtypeStruct((t, d), F32),
        scratch_shapes=[pltpu.VMEM((tm, d), F32)],
        compiler_params=_params(("parallel",)),
        name="mix_out_ffn",
    )(h, olat, of, bdv, wo, g, wg, wu, wd, fg)


def _cast_kernel(x_ref, o_ref):
    o_ref[...] = x_ref[...].astype(o_ref.dtype)


def _to_bf16(x, *, rows=512):
    x2 = x.reshape(-1, x.shape[-1])
    r, c = x2.shape
    rows = math.gcd(r, rows)
    assert rows % (2 * SUBLANES) == 0
    out = pl.pallas_call(
        _cast_kernel,
        grid=(r // rows,),
        in_specs=[pl.BlockSpec((rows, c), lambda i: (i, 0))],
        out_specs=pl.BlockSpec((rows, c), lambda i: (i, 0)),
        out_shape=jax.ShapeDtypeStruct((r, c), BF16),
        compiler_params=_params(("parallel",)),
        name="to_bf16",
    )(x2)
    return out.reshape(x.shape)


def _pad_cols(w, width):
    return jnp.pad(w, ((0, 0), (0, 0), (0, width - w.shape[2])))


def _mixer_weights(w_in, w_uk, w_uv, n_fox):
    _, d_lat, n_dsa, hd = w_uk.shape
    w_dsa = n_dsa * hd
    w_idx = H_IDX * D_IDX
    w_fox = n_fox * hd
    sizes = [w_dsa, d_lat, w_idx, D_IDX, H_IDX, 3 * w_fox, n_fox]
    assert sum(sizes) == w_in.shape[2] and d_lat == LANES and 2 * D_IDX == LANES
    offs = [0]
    for sz in sizes:
        offs.append(offs[-1] + sz)
    q_a, c_raw, q_i, k_i, w_i, qkv_f, f_l = [w_in[:, :, offs[j]:offs[j + 1]] for j in range(7)]
    w2 = jnp.concatenate(
        [q_a, c_raw, q_i, k_i, k_i, _pad_cols(w_i, LANES), _pad_cols(f_l, LANES), qkv_f], axis=2)
    depth = w_in.shape[0]
    eye = jnp.eye(n_dsa, dtype=w_uk.dtype)
    bdk = jnp.einsum('lchd,hg->lhdgc', w_uk, eye).reshape(depth, w_dsa, n_dsa * d_lat).astype(BF16)
    bdv = jnp.einsum('lchd,hg->lhcgd', w_uv, eye).reshape(depth, n_dsa * d_lat, w_dsa).astype(BF16)
    return w2, bdk, bdv, dict(w_dsa=w_dsa, w_idx=w_idx, w_fox=w_fox, d_lat=d_lat)


def kernel(x, ffn1_norm, ffn1_gate, ffn1_up, ffn1_down, mix_norm, w_in, b_forget, ckv_norm, w_uk, w_uv, w_out,
           ffn2_norm, ffn2_gate, ffn2_up, ffn2_down, rel_bias, final_norm):
    b, s, d = x.shape
    depth = w_in.shape[0]
    n_fox = b_forget.shape[1]
    t = b * s
    tile = min(DSA_TILE, s)
    fox_tile = min(FOX_TILE, s)
    k_sel = min(TOPK_MAX, s // 4)
    npair = n_fox * HEAD_DIM // LANES

    bias_tiles = _bias_tiles(rel_bias, tile)
    fg = final_norm.reshape(1, d)

    g1, u1, d1 = _to_bf16(ffn1_gate), _to_bf16(ffn1_up), _to_bf16(ffn1_down)
    g2, u2, d2 = _to_bf16(ffn2_gate), _to_bf16(ffn2_up), _to_bf16(ffn2_down)
    wo = _to_bf16(w_out)
    w2, bdk, bdv, dims = _mixer_weights(_to_bf16(w_in), w_uk, w_uv, n_fox)
    n1, nm, n2 = (v[:, None, :] for v in (ffn1_norm, mix_norm, ffn2_norm))
    gc, bfg = ckv_norm[:, None, :], b_forget[:, None, :]

    h = x.reshape(t, d)
    for l in range(depth):
        h = _ffn(h, n1, g1, u1, d1, l)
        qlat, ckv, qi, ki2, wi, logf, qf, kf, vf = _mix_in(h, nm, w2, gc, bdk, bfg, l, **dims)

        cum_row = _cumsum_rows(jnp.swapaxes(logf.reshape(b, s, n_fox), 1, 2).reshape(b * n_fox, s))
        cum_row = cum_row.reshape(b, npair, 2, s)

        olat = _dsa(qlat.reshape(-1, b, s, LANES), qi.reshape(b, s, -1), wi.reshape(b, s, -1),
                    ki2.reshape(b, s, -1), ckv.reshape(b, s, -1), bias_tiles, k_sel=k_sel, tile=tile)
        of = _fox(qf.reshape(npair, b, s, LANES), kf.reshape(npair, b, s, LANES), vf.reshape(npair, b, s, LANES),
                  cum_row, tile=fox_tile)

        h = _mix_out_ffn(h, olat.reshape(t, -1), of.reshape(npair, t, LANES), bdv, wo, n2, g2, u2, d2, fg, l,
                         final=(l == depth - 1))
    return h.reshape(b, s, d)
```

```python
import functools
import math

import jax
import jax.numpy as jnp
from jax import lax
from jax.experimental import pallas as pl
from jax.experimental.pallas import tpu as pltpu

HEAD_DIM = 64
H_IDX = 8
D_IDX = 64
TOPK_MAX = 256
MAX_DISTANCE = 128
EPS = 1e-6
NEG = -1e30

LANES = 128
SUBLANES = 8
V7X_VMEM_LIMIT = 56 * 1024 * 1024

DSA_TILE = 256
FOX_TILE = 256
ATTN_STRIP = 32
FAR_TILES = 2
N_BIAS_TILES = 3

LOG2E = math.log2(math.e)
INT_MIN = -(2 ** 31)
F32 = jnp.float32
BF16 = jnp.bfloat16
I32 = jnp.int32

_NT = (((1,), (1,)), ((), ()))


def _params(sem):
    return pltpu.CompilerParams(dimension_semantics=sem, vmem_limit_bytes=V7X_VMEM_LIMIT)


def _resident(shape):
    zeros = (0,) * len(shape)
    return pl.BlockSpec(shape, lambda *_: zeros, pipeline_mode=pl.Buffered(1))


def _layer(stacked, l):
    _, r, c = stacked.shape
    return pl.BlockSpec((None, r, c), lambda *_: (l, 0, 0), pipeline_mode=pl.Buffered(1))


def _rms(x, g):
    y = x * lax.rsqrt(jnp.mean(x * x, axis=-1, keepdims=True) + EPS)
    return y * g


def _swiglu_half_step(x, g_ref, wg_ref, wu_ref, wd_ref, acc_ref, ff_chunk):
    xn = _rms(x, g_ref[...]).astype(BF16)
    d_ff = wg_ref.shape[1]
    for c0 in range(0, d_ff, ff_chunk):
        c1 = min(c0 + ff_chunk, d_ff)
        g = jnp.dot(xn, wg_ref[:, c0:c1], preferred_element_type=F32)
        u = jnp.dot(xn, wu_ref[:, c0:c1], preferred_element_type=F32)
        a = (g * jax.nn.sigmoid(g) * u).astype(BF16)
        d = jnp.dot(a, wd_ref[c0:c1, :], preferred_element_type=F32)
        if c0 == 0:
            acc_ref[...] = d
        else:
            acc_ref[...] += d
    return x + 0.5 * acc_ref[...]


def _ffn_kernel(x_ref, g_ref, wg_ref, wu_ref, wd_ref, o_ref, acc_ref, *, ff_chunk):
    o_ref[...] = _swiglu_half_step(x_ref[...], g_ref, wg_ref, wu_ref, wd_ref, acc_ref, ff_chunk)


def _ffn(h, g, wg, wu, wd, l, *, tm=512, ff_chunk=256):
    t, d = h.shape
    tm = min(tm, t)
    return pl.pallas_call(
        functools.partial(_ffn_kernel, ff_chunk=ff_chunk),
        grid=(t // tm,),
        in_specs=[
            pl.BlockSpec((tm, d), lambda i: (i, 0)),
            _layer(g, l),
            _layer(wg, l),
            _layer(wu, l),
            _layer(wd, l),
        ],
        out_specs=pl.BlockSpec((tm, d), lambda i: (i, 0)),
        out_shape=jax.ShapeDtypeStruct((t, d), F32),
        scratch_shapes=[pltpu.VMEM((tm, d), F32)],
        compiler_params=_params(("parallel",)),
        name="ffn",
    )(h, g, wg, wu, wd)


def _mix_in_kernel(x_ref, g_ref, w_ref, gc_ref, bdk_ref, bf_ref,
                   qlat_ref, ckv_ref, qi_ref, ki_ref, wi_ref, lf_ref, qf_ref, kf_ref, vf_ref,
                   *, w_dsa, w_idx, n_idx_heads, w_fox):
    xn = _rms(x_ref[...], g_ref[...]).astype(BF16)
    pos = [0]

    def proj(width):
        c0 = pos[0]
        pos[0] = c0 + width
        return jnp.dot(xn, w_ref[:, c0:c0 + width], preferred_element_type=F32)

    qk_scale = HEAD_DIM ** -0.5 * LOG2E
    idx_scale = D_IDX ** -0.5
    qa = proj(w_dsa).astype(BF16)
    qlat = jnp.dot(qa, bdk_ref[...], preferred_element_type=F32)
    for h in range(qlat_ref.shape[0]):
        qlat_ref[h] = (qlat[:, h * LANES:(h + 1) * LANES] * qk_scale).astype(BF16)
    ckv_ref[...] = _rms(proj(LANES), gc_ref[...]).astype(BF16)
    qi_ref[...] = (proj(w_idx) * idx_scale).astype(BF16)
    ki_ref[...] = proj(LANES).astype(BF16)
    wi_ref[...] = proj(LANES)[:, :n_idx_heads] * (n_idx_heads ** -0.5)
    n_fox = bf_ref.shape[1]
    f = proj(LANES)[:, :n_fox] + bf_ref[...]
    lf_ref[...] = jnp.minimum(f, 0.0) - jnp.log1p(jnp.exp(-jnp.abs(f)))
    for hp in range(w_fox // LANES):
        qf_ref[hp] = (proj(LANES) * qk_scale).astype(BF16)
    for hp in range(w_fox // LANES):
        kf_ref[hp] = proj(LANES).astype(BF16)
    for hp in range(w_fox // LANES):
        vf_ref[hp] = proj(LANES).astype(BF16)


def _mix_in(h, g, w2, gc, bdk, bfg, l, *, w_dsa, w_idx, w_fox, d_lat, tm=512):
    t, d = h.shape
    tm = min(tm, t)
    n_dsa = w_dsa // HEAD_DIM
    n_fox = bfg.shape[2]
    npair = w_fox // LANES
    row = lambda i: (i, 0)
    row3 = lambda i: (0, i, 0)
    out_shape = [
        jax.ShapeDtypeStruct((n_dsa, t, d_lat), BF16),
        jax.ShapeDtypeStruct((t, d_lat), BF16),
        jax.ShapeDtypeStruct((t, w_idx), BF16),
        jax.ShapeDtypeStruct((t, LANES), BF16),
        jax.ShapeDtypeStruct((t, H_IDX), F32),
        jax.ShapeDtypeStruct((t, n_fox), F32),
        jax.ShapeDtypeStruct((npair, t, LANES), BF16),
        jax.ShapeDtypeStruct((npair, t, LANES), BF16),
        jax.ShapeDtypeStruct((npair, t, LANES), BF16),
    ]
    out_specs = [
        pl.BlockSpec((n_dsa, tm, d_lat), row3),
        pl.BlockSpec((tm, d_lat), row),
        pl.BlockSpec((tm, w_idx), row),
        pl.BlockSpec((tm, LANES), row),
        pl.BlockSpec((tm, H_IDX), row),
        pl.BlockSpec((tm, n_fox), row),
        pl.BlockSpec((npair, tm, LANES), row3),
        pl.BlockSpec((npair, tm, LANES), row3),
        pl.BlockSpec((npair, tm, LANES), row3),
    ]
    return pl.pallas_call(
        functools.partial(_mix_in_kernel, w_dsa=w_dsa, w_idx=w_idx, n_idx_heads=H_IDX, w_fox=w_fox),
        grid=(t // tm,),
        in_specs=[
            pl.BlockSpec((tm, d), row),
            _layer(g, l),
            _layer(w2, l),
            _layer(gc, l),
            _layer(bdk, l),
            _layer(bfg, l),
        ],
        out_specs=out_specs,
        out_shape=out_shape,
        compiler_params=_params(("parallel",)),
        name="mix_in",
    )(h, g, w2, gc, bdk, bfg)


def _cumsum_kernel(x_ref, o_ref):
    x = x_ref[...]
    n = x.shape[1]
    lane = lax.broadcasted_iota(I32, x.shape, 1)
    sh = 1
    while sh < n:
        x = x + jnp.where(lane >= sh, pltpu.roll(x, sh, axis=1), 0.0)
        sh *= 2
    o_ref[...] = x * LOG2E


def _cumsum_rows(x):
    r, n = x.shape
    rb = 8 if r % 8 == 0 else r
    return pl.pallas_call(
        _cumsum_kernel,
        grid=(r // rb,),
        in_specs=[pl.BlockSpec((rb, n), lambda i: (i, 0))],
        out_specs=pl.BlockSpec((rb, n), lambda i: (i, 0)),
        out_shape=jax.ShapeDtypeStruct((r, n), F32),
        compiler_params=_params(("parallel",)),
        name="cumsum",
    )(x)


def _bias_tiles_kernel(rb_ref, bucket_ref, o_ref):
    n_buckets, n_heads = rb_ref.shape
    for j in range(bucket_ref.shape[0]):
        bucket = bucket_ref[j]
        for h in range(n_heads):
            acc = jnp.zeros(bucket.shape, F32)
            for b in range(n_buckets):
                acc = jnp.where(bucket == b, rb_ref[b, h], acc)
            o_ref[j, h] = (acc - rb_ref[n_buckets - 1, h]) * LOG2E


def _t5_bucket(dist, n_buckets):
    n = jnp.maximum(dist, 0)
    max_exact = n_buckets // 2
    nf = jnp.maximum(n, 1).astype(F32)
    large = max_exact + (jnp.log(nf / max_exact) / math.log(MAX_DISTANCE / max_exact)
                         * (n_buckets - max_exact)).astype(I32)
    large = jnp.minimum(large, n_buckets - 1)
    return jnp.where(n < max_exact, n, large)


def _bias_tiles(rel_bias, tile):
    n_buckets, n_heads = rel_bias.shape
    assert tile >= MAX_DISTANCE
    assert N_BIAS_TILES == FAR_TILES + 1
    r = lax.broadcasted_iota(I32, (tile, tile), 0)
    c = lax.broadcasted_iota(I32, (tile, tile), 1)
    buckets = jnp.stack([_t5_bucket(j * tile + r - c, n_buckets) for j in range(N_BIAS_TILES)])
    return pl.pallas_call(
        _bias_tiles_kernel,
        in_specs=[pl.BlockSpec(memory_space=pltpu.SMEM), pl.BlockSpec(memory_space=pltpu.VMEM)],
        out_specs=pl.BlockSpec(memory_space=pltpu.VMEM),
        out_shape=jax.ShapeDtypeStruct((N_BIAS_TILES, n_heads, tile, tile), F32),
        compiler_params=pltpu.CompilerParams(vmem_limit_bytes=V7X_VMEM_LIMIT),
        name="bias_tiles",
    )(rel_bias, buckets)


def _softmax_strip(s, rows, p_ref, alpha_ref, m_ref, l_ref):
    width = s.shape[1]
    reps = width // LANES
    m_old = m_ref[rows, :]
    m_new = jnp.maximum(m_old, jnp.max(s, axis=1, keepdims=True))
    alpha = jnp.exp2(m_old - m_new)
    p = jnp.exp2(s - jnp.tile(m_new, (1, reps)))
    psum = p[:, 0:LANES]
    for j in range(1, reps):
        psum = psum + p[:, j * LANES:(j + 1) * LANES]
    l_ref[rows, :] = alpha * l_ref[rows, :] + psum
    alpha_ref[rows, :] = alpha
    m_ref[rows, :] = m_new
    p_ref[rows, 0:width] = p.astype(BF16)


def _dsa_kernel(qlat_ref, qi_ref, wi_ref, ki_ref, ckv_ref, bias_ref, o_ref,
                keys_ref, keyst_ref, thr_ref, last_ref, qidx_ref,
                s_ref, p_ref, alpha_ref, m_ref, l_ref, acc_ref,
                *, tile, k_sel, seq, strip):
    n_heads, _, d_lat = qlat_ref.shape
    i = pl.program_id(1)
    n_chunks = i + 1
    row = lax.broadcasted_iota(I32, (tile, tile), 0)
    col = lax.broadcasted_iota(I32, (tile, tile), 1)
    diff = col - row
    lane = lax.broadcasted_iota(I32, (tile, LANES), 1)
    low_half = lane < D_IDX

    def chunk_start(kc):
        return pl.multiple_of(kc * tile, tile)

    qi = qi_ref[...]
    wi_t = wi_ref[...]
    zero = jnp.zeros((), BF16)
    for h in range(H_IDX):
        grp = qi[:, (h // 2) * LANES:(h // 2 + 1) * LANES]
        qidx_ref[h * tile:(h + 1) * tile, :] = (jnp.where(low_half, grp, zero) if h % 2 == 0
                                                 else jnp.where(low_half, zero, grp))

    def score_chunk(start, width, j):
        k2 = ki_ref[pl.ds(start, width), :]
        acc = None
        for h in range(H_IDX):
            r = lax.dot_general(k2, qidx_ref[h * tile:(h + 1) * tile, :], _NT, preferred_element_type=F32)
            term = jnp.maximum(r, 0.0) * wi_t[h:h + 1, :]
            acc = term if acc is None else acc + term
        if j is not None:
            acc = jnp.where(row - col <= j * tile, acc, NEG)
        bits = pltpu.bitcast(acc, I32)
        key_t = jnp.where(bits < 0, INT_MIN - bits, bits)
        keyst_ref[pl.ds(start, width), :] = key_t
        keys_ref[:, pl.ds(start, width)] = key_t.T

    score_width = FAR_TILES * tile

    def score_far(g, carry):
        score_chunk(pl.multiple_of(g * score_width, score_width), score_width, None)
        return carry

    def score_near(kc, carry):
        score_chunk(chunk_start(kc), tile, i - kc)
        return carry

    n_score_groups = i // FAR_TILES
    lax.fori_loop(0, n_score_groups, score_far, 0)
    lax.fori_loop(n_score_groups * FAR_TILES, i + 1, score_near, 0)

    groups = tile // SUBLANES

    def key_tile(c):
        return keyst_ref[pl.ds(chunk_start(c), tile), :].reshape(groups, SUBLANES, tile)

    def count(pred):
        def one(c, cnt):
            return cnt + jnp.sum(pred(key_tile(c), c).astype(I32), axis=0)

        def two(g, cnt):
            return one(2 * g + 1, one(2 * g, cnt))

        cnt = lax.fori_loop(0, n_chunks // 2, two, jnp.zeros((SUBLANES, tile), I32))
        cnt = lax.fori_loop(2 * (n_chunks // 2), n_chunks, one, cnt)
        return jnp.broadcast_to(jnp.sum(cnt, axis=0, keepdims=True), (SUBLANES, tile))

    def to_rows(x):
        return jnp.broadcast_to(x[0:1, :], (LANES, tile)).T

    n_nonneg = count(lambda k, c: k >= 0)
    start_high = n_nonneg >= k_sel
    prefix = jnp.where(start_high, 0, INT_MIN).astype(I32)
    n_ge = jnp.where(start_high, n_nonneg, n_chunks * tile)

    def bit_body(it, state):
        prefix, n_ge = state
        cand = prefix | lax.shift_left(jnp.int32(1), 30 - it)
        cnt = count(lambda k, c: k >= cand[None])
        keep = cnt >= k_sel
        return jnp.where(keep, cand, prefix), jnp.where(keep, cnt, n_ge)

    thr, n_ge = lax.fori_loop(0, 31, bit_body, (prefix, n_ge))
    thr_ref[...] = to_rows(thr)


    @pl.when(jnp.max(n_ge) > k_sel)
    def _():
        need = k_sel - count(lambda k, c: k > thr[None])
        pos_bits = max(seq - 1, 1).bit_length()
        pos = row.reshape(groups, SUBLANES, tile)

        def pos_body(it, x):
            cand = x | lax.shift_left(jnp.int32(1), pos_bits - 1 - it)
            f = count(lambda k, c: (k == thr[None]) & (pos + c * tile < cand[None]))
            return jnp.where(f < need, cand, x)

        last = lax.fori_loop(0, pos_bits, pos_body, jnp.zeros((SUBLANES, tile), I32))
        last_ref[...] = to_rows(last)
        reps = tile // LANES

        def drop_body(c, carry):
            cols = pl.ds(chunk_start(c), tile)
            k = keys_ref[:, cols]
            drop = (k == jnp.tile(thr_ref[...], (1, reps))) & (col + c * tile > jnp.tile(last_ref[...], (1, reps)))
            keys_ref[:, cols] = jnp.where(drop, INT_MIN, k)
            return carry

        lax.fori_loop(0, n_chunks, drop_body, 0)

    m_ref[...] = jnp.full(m_ref.shape, NEG, F32)
    l_ref[...] = jnp.zeros(l_ref.shape, F32)
    acc_ref[...] = jnp.zeros(acc_ref.shape, F32)
    q_all = qlat_ref[...].reshape(n_heads * tile, d_lat)
    sdiff = diff[0:strip, :]

    def attn_chunk(start, width, j):
        cols = pl.ds(start, width)
        c = ckv_ref[cols, :]
        s_ref[:, 0:width] = lax.dot_general(q_all, c, _NT, preferred_element_type=F32)
        for r0 in range(0, tile, strip):
            qrows = pl.ds(r0, strip)
            sel = keys_ref[qrows, cols] >= jnp.tile(thr_ref[qrows, :], (1, width // LANES))
            if j is not None:
                sel = sel & (sdiff - r0 <= j * tile)
            penalty = jnp.where(sel, 0.0, NEG)
            for h in range(n_heads):
                rows = pl.ds(h * tile + r0, strip)
                s = s_ref[rows, 0:width] + penalty
                if j is not None:
                    s = s + bias_ref[j, h, qrows, :]
                _softmax_strip(s, rows, p_ref, alpha_ref, m_ref, l_ref)
        acc_ref[...] = alpha_ref[...] * acc_ref[...] + jnp.dot(p_ref[:, 0:width], c,
                                                                preferred_element_type=F32)

    far_width = FAR_TILES * tile

    def far_body(g, carry):
        attn_chunk(pl.multiple_of(g * far_width, far_width), far_width, None)
        return carry

    def near_body(kc, carry):
        attn_chunk(chunk_start(kc), tile, i - kc)
        return carry

    n_far_groups = jnp.maximum(i - 1, 0) // FAR_TILES
    lax.fori_loop(0, n_far_groups, far_body, 0)
    lax.fori_loop(n_far_groups * FAR_TILES, i + 1, near_body, 0)

    for h in range(n_heads):
        rows = slice(h * tile, (h + 1) * tile)
        l = jnp.sum(l_ref[rows, :], axis=1, keepdims=True)
        o_ref[:, h * d_lat:(h + 1) * d_lat] = (acc_ref[rows, :] / l).astype(BF16)


def _dsa(qlat, qi, wi, ki2, ckv, bias_tiles, *, k_sel, tile):
    n_heads, b, s, d_lat = qlat.shape
    assert d_lat == LANES
    w_idx = qi.shape[2]
    qblk = lambda bi, i: (bi, i, 0)
    whole = lambda bi, i: (bi, 0, 0)
    return pl.pallas_call(
        functools.partial(_dsa_kernel, tile=tile, k_sel=k_sel, seq=s, strip=min(ATTN_STRIP, tile)),
        grid=(b, s // tile),
        in_specs=[
            pl.BlockSpec((n_heads, None, tile, d_lat), lambda bi, i: (0, bi, i, 0)),
            pl.BlockSpec((None, tile, w_idx), qblk),
            pl.BlockSpec((None, H_IDX, tile), lambda bi, i: (bi, 0, i)),
            pl.BlockSpec((None, s, LANES), whole),
            pl.BlockSpec((None, s, d_lat), whole),
            _resident(bias_tiles.shape),
        ],
        out_specs=pl.BlockSpec((None, tile, n_heads * d_lat), qblk),
        out_shape=jax.ShapeDtypeStruct((b, s, n_heads * d_lat), BF16),
        scratch_shapes=[
            pltpu.VMEM((tile, s), I32),
            pltpu.VMEM((s, tile), I32),
            pltpu.VMEM((tile, LANES), I32),
            pltpu.VMEM((tile, LANES), I32),
            pltpu.VMEM((H_IDX * tile, LANES), BF16),
            pltpu.VMEM((n_heads * tile, FAR_TILES * tile), F32),
            pltpu.VMEM((n_heads * tile, FAR_TILES * tile), BF16),
            pltpu.VMEM((n_heads * tile, LANES), F32),
            pltpu.VMEM((n_heads * tile, LANES), F32),
            pltpu.VMEM((n_heads * tile, LANES), F32),
            pltpu.VMEM((n_heads * tile, d_lat), F32),
        ],
        compiler_params=_params(("parallel", "arbitrary")),
        name="dsa",
    )(qlat, qi, wi, ki2, ckv, bias_tiles)


def _fox_kernel(q_ref, k_ref, v_ref, cq_ref, ck_ref, o_ref,
                s_ref, p_ref, cqr_ref, alpha_ref, m_ref, l_ref, acc_ref, *, tile, strip):
    n_pairs = q_ref.shape[0]
    i = pl.program_id(1)
    sdiff = (lax.broadcasted_iota(I32, (strip, tile), 1)
             - lax.broadcasted_iota(I32, (strip, tile), 0))
    lane = lax.broadcasted_iota(I32, (tile, LANES), 1)
    low_half = lane < HEAD_DIM
    zero = jnp.zeros((), BF16)
    q_all = []
    for pr in range(n_pairs):
        q = q_ref[pr]
        q_all.append(jnp.concatenate([jnp.where(low_half, q, zero), jnp.where(low_half, zero, q)], axis=0))
        for h in range(2):
            cqr_ref[(2 * pr + h) * tile:(2 * pr + h + 1) * tile, :] = jnp.broadcast_to(
                cq_ref[pr, h:h + 1, :], (LANES, tile)).T
    m_ref[...] = jnp.full(m_ref.shape, NEG, F32)
    l_ref[...] = jnp.zeros(l_ref.shape, F32)
    acc_ref[...] = jnp.zeros(acc_ref.shape, F32)

    def chunk(start, width, j):
        cols = pl.ds(start, width)
        for pr in range(n_pairs):
            s_ref[pr * 2 * tile:(pr + 1) * 2 * tile, 0:width] = lax.dot_general(
                q_all[pr], k_ref[pr, cols, :], _NT, preferred_element_type=F32)
        ck = [ck_ref[pr, :, cols] for pr in range(n_pairs)]
        for r0 in range(0, tile, strip):
            if j is not None:
                penalty = jnp.where(sdiff - r0 <= j * tile, 0.0, NEG)
            for pr in range(n_pairs):
                for h in range(2):
                    rows = pl.ds((2 * pr + h) * tile + r0, strip)
                    s = s_ref[rows, 0:width] + jnp.tile(cqr_ref[rows, :], (1, width // LANES)) - ck[pr][h:h + 1, :]
                    if j is not None:
                        s = s + penalty
                    _softmax_strip(s, rows, p_ref, alpha_ref, m_ref, l_ref)
        for pr in range(n_pairs):
            rows = slice(pr * 2 * tile, (pr + 1) * 2 * tile)
            acc_ref[rows, :] = alpha_ref[rows, :] * acc_ref[rows, :] + jnp.dot(
                p_ref[rows, 0:width], v_ref[pr, cols, :], preferred_element_type=F32)

    far_width = FAR_TILES * tile

    def far_body(g, carry):
        chunk(pl.multiple_of(g * far_width, far_width), far_width, None)
        return carry

    def near_body(kc, carry):
        chunk(pl.multiple_of(kc * tile, tile), tile, i - kc)
        return carry

    n_far_groups = i // FAR_TILES
    lax.fori_loop(0, n_far_groups, far_body, 0)
    lax.fori_loop(n_far_groups * FAR_TILES, i + 1, near_body, 0)
    for pr in range(n_pairs):
        r0 = 2 * pr * tile
        o0 = acc_ref[r0:r0 + tile, :] / jnp.sum(l_ref[r0:r0 + tile, :], axis=1, keepdims=True)
        o1 = acc_ref[r0 + tile:r0 + 2 * tile, :] / jnp.sum(l_ref[r0 + tile:r0 + 2 * tile, :], axis=1, keepdims=True)
        o_ref[pr] = jnp.where(low_half, o0, o1).astype(BF16)


def _fox(qf, kf, vf, cum, *, tile):
    npair, b, s, _ = qf.shape
    rows = 2 * npair * tile
    qblk = lambda bi, i: (0, bi, i, 0)
    whole = lambda bi, i: (0, bi, 0, 0)
    return pl.pallas_call(
        functools.partial(_fox_kernel, tile=tile, strip=min(ATTN_STRIP, tile)),
        grid=(b, s // tile),
        in_specs=[
            pl.BlockSpec((npair, None, tile, LANES), qblk),
            pl.BlockSpec((npair, None, s, LANES), whole),
            pl.BlockSpec((npair, None, s, LANES), whole),
            pl.BlockSpec((None, npair, 2, tile), lambda bi, i: (bi, 0, 0, i)),
            pl.BlockSpec((None, npair, 2, s), lambda bi, i: (bi, 0, 0, 0)),
        ],
        out_specs=pl.BlockSpec((npair, None, tile, LANES), qblk),
        out_shape=jax.ShapeDtypeStruct((npair, b, s, LANES), BF16),
        scratch_shapes=[
            pltpu.VMEM((rows, FAR_TILES * tile), F32),
            pltpu.VMEM((rows, FAR_TILES * tile), BF16),
            pltpu.VMEM((rows, LANES), F32),
            pltpu.VMEM((rows, LANES), F32),
            pltpu.VMEM((rows, LANES), F32),
            pltpu.VMEM((rows, LANES), F32),
            pltpu.VMEM((rows, LANES), F32),
        ],
        compiler_params=_params(("parallel", "arbitrary")),
        name="fox",
    )(qf, kf, vf, cum, cum)


def _mix_out_ffn_kernel(h_ref, ol_ref, of_ref, bdv_ref, wo_ref, g_ref, wg_ref, wu_ref, wd_ref, fg_ref,
                        o_ref, acc_ref, *, ff_chunk, final):
    oa = jnp.dot(ol_ref[...], bdv_ref[...], preferred_element_type=F32).astype(BF16)
    w_dsa = oa.shape[1]
    y = jnp.dot(oa, wo_ref[0:w_dsa, :], preferred_element_type=F32)
    for hp in range(of_ref.shape[0]):
        r0 = w_dsa + hp * LANES
        y = y + jnp.dot(of_ref[hp], wo_ref[r0:r0 + LANES, :], preferred_element_type=F32)
    out = _swiglu_half_step(h_ref[...] + y, g_ref, wg_ref, wu_ref, wd_ref, acc_ref, ff_chunk)
    if final:
        out = _rms(out, fg_ref[...])
    o_ref[...] = out


def _mix_out_ffn(h, olat, of, bdv, wo, g, wg, wu, wd, fg, l, *, final, tm=512, ff_chunk=256):
    t, d = h.shape
    tm = min(tm, t)
    npair = of.shape[0]
    row = lambda i: (i, 0)
    return pl.pallas_call(
        functools.partial(_mix_out_ffn_kernel, ff_chunk=ff_chunk, final=final),
        grid=(t // tm,),
        in_specs=[
            pl.BlockSpec((tm, d), row),
            pl.BlockSpec((tm, olat.shape[1]), row),
            pl.BlockSpec((npair, tm, LANES), lambda i: (0, i, 0)),
            _layer(bdv, l),
            _layer(wo, l),
            _layer(g, l),
            _layer(wg, l),
            _layer(wu, l),
            _layer(wd, l),
            _resident((1, d)),
        ],
        out_specs=pl.BlockSpec((tm, d), row),
        out_shape=jax.ShapeDtypeStruct((t, d), F32),
        scratch_shapes=[pltpu.VMEM((tm, d), F32)],
        compiler_params=_params(("parallel",)),
        name="mix_out_ffn",
    )(h, olat, of, bdv, wo, g, wg, wu, wd, fg)


def _cast_kernel(x_ref, o_ref):
    o_ref[...] = x_ref[...].astype(o_ref.dtype)


def _to_bf16(x, *, rows=512):
    x2 = x.reshape(-1, x.shape[-1])
    r, c = x2.shape
    rows = math.gcd(r, rows)
    assert rows % (2 * SUBLANES) == 0
    out = pl.pallas_call(
        _cast_kernel,
        grid=(r // rows,),
        in_specs=[pl.BlockSpec((rows, c), lambda i: (i, 0))],
        out_specs=pl.BlockSpec((rows, c), lambda i: (i, 0)),
        out_shape=jax.ShapeDtypeStruct((r, c), BF16),
        compiler_params=_params(("parallel",)),
        name="to_bf16",
    )(x2)
    return out.reshape(x.shape)


def _pad_cols(w, width):
    return jnp.pad(w, ((0, 0), (0, 0), (0, width - w.shape[2])))


def _mixer_weights(w_in, w_uk, w_uv, n_fox):
    _, d_lat, n_dsa, hd = w_uk.shape
    w_dsa = n_dsa * hd
    w_idx = H_IDX * D_IDX
    w_fox = n_fox * hd
    sizes = [w_dsa, d_lat, w_idx, D_IDX, H_IDX, 3 * w_fox, n_fox]
    assert sum(sizes) == w_in.shape[2] and d_lat == LANES and 2 * D_IDX == LANES
    offs = [0]
    for sz in sizes:
        offs.append(offs[-1] + sz)
    q_a, c_raw, q_i, k_i, w_i, qkv_f, f_l = [w_in[:, :, offs[j]:offs[j + 1]] for j in range(7)]
    w2 = jnp.concatenate(
        [q_a, c_raw, q_i, k_i, k_i, _pad_cols(w_i, LANES), _pad_cols(f_l, LANES), qkv_f], axis=2)
    depth = w_in.shape[0]
    eye = jnp.eye(n_dsa, dtype=w_uk.dtype)
    bdk = jnp.einsum('lchd,hg->lhdgc', w_uk, eye).reshape(depth, w_dsa, n_dsa * d_lat).astype(BF16)
    bdv = jnp.einsum('lchd,hg->lhcgd', w_uv, eye).reshape(depth, n_dsa * d_lat, w_dsa).astype(BF16)
    return w2, bdk, bdv, dict(w_dsa=w_dsa, w_idx=w_idx, w_fox=w_fox, d_lat=d_lat)


def kernel(x, ffn1_norm, ffn1_gate, ffn1_up, ffn1_down, mix_norm, w_in, b_forget, ckv_norm, w_uk, w_uv, w_out,
           ffn2_norm, ffn2_gate, ffn2_up, ffn2_down, rel_bias, final_norm):
    b, s, d = x.shape
    depth = w_in.shape[0]
    n_fox = b_forget.shape[1]
    t = b * s
    tile = min(DSA_TILE, s)
    fox_tile = min(FOX_TILE, s)
    k_sel = min(TOPK_MAX, s // 4)
    npair = n_fox * HEAD_DIM // LANES

    bias_tiles = _bias_tiles(rel_bias, tile)
    fg = final_norm.reshape(1, d)

    g1, u1, d1 = _to_bf16(ffn1_gate), _to_bf16(ffn1_up), _to_bf16(ffn1_down)
    g2, u2, d2 = _to_bf16(ffn2_gate), _to_bf16(ffn2_up), _to_bf16(ffn2_down)
    wo = _to_bf16(w_out)
    w2, bdk, bdv, dims = _mixer_weights(_to_bf16(w_in), w_uk, w_uv, n_fox)
    n1, nm, n2 = (v[:, None, :] for v in (ffn1_norm, mix_norm, ffn2_norm))
    gc, bfg = ckv_norm[:, None, :], b_forget[:, None, :]

    h = x.reshape(t, d)
    for l in range(depth):
        h = _ffn(h, n1, g1, u1, d1, l)
        qlat, ckv, qi, ki2, wi, logf, qf, kf, vf = _mix_in(h, nm, w2, gc, bdk, bfg, l, **dims)

        cum_row = _cumsum_rows(jnp.swapaxes(logf.reshape(b, s, n_fox), 1, 2).reshape(b * n_fox, s))
        cum_row = cum_row.reshape(b, npair, 2, s)

        olat = _dsa(qlat.reshape(-1, b, s, LANES), qi.reshape(b, s, -1),
                    jnp.swapaxes(wi.reshape(b, s, -1), 1, 2),
                    ki2.reshape(b, s, -1), ckv.reshape(b, s, -1), bias_tiles, k_sel=k_sel, tile=tile)
        of = _fox(qf.reshape(npair, b, s, LANES), kf.reshape(npair, b, s, LANES), vf.reshape(npair, b, s, LANES),
                  cum_row, tile=fox_tile)

        h = _mix_out_ffn(h, olat.reshape(t, -1), of.reshape(npair, t, LANES), bdv, wo, n2, g2, u2, d2, fg, l,
                         final=(l == depth - 1))
    return h.reshape(b, s, d)
```
